```python
import functools
import jax, jax.numpy as jnp
from jax import lax
import numpy as np

D_MODEL = 1024
BATCH = 16
SEQ = 2048
DEPTH = 1
DEC_BATCH = 128
DEC_SEQ = 1
PAST_LEN = 16384
PAGE_SIZE = 128

HEAD_DIM = 64
SWA_WINDOW = 128
SWA_Q_HEADS = 8
SWA_KV_HEADS = 2
SWA_GROUP = SWA_Q_HEADS // SWA_KV_HEADS
DIL_PAIRS = ((128, 1), (512, 4), (2048, 16))
N_DIL = len(DIL_PAIRS)
DIL_HEADS = 4
BLOCK = 128
Q_A = SWA_Q_HEADS * HEAD_DIM
KV_A = SWA_KV_HEADS * HEAD_DIM
QKV_B = N_DIL * DIL_HEADS * HEAD_DIM
O_B = DIL_HEADS * HEAD_DIM
N_IN = Q_A + 2 * KV_A + 3 * QKV_B + 2 * D_MODEL
N_EXPERTS = 64
TOP_K = 8
N_EXPERT_GROUPS = 8
EXPERTS_PER_GROUP = N_EXPERTS // N_EXPERT_GROUPS
TOPK_GROUPS = 4
D_EXPERT = 256
D_SHARED = 256
ROUTED_SCALE = 2.5
MOE_BLOCK = 128
LN_EPS = 1e-5
NEG_INF = -1e30
ALPHA = (2.0 * DEPTH) ** 0.25
BETA = (8.0 * DEPTH) ** -0.25

kernel_name = 'hybrid_swa_dilated_moe_decoder_step'


def _alibi_slopes(n):
    return jnp.asarray(2.0 ** (-8.0 * np.arange(1, n + 1) / n), dtype=jnp.float32)


def _layer_norm(x, gain=None, bias=None):
    xf = x.astype(jnp.float32)
    mu = xf.mean(-1, keepdims=True)
    var = jnp.square(xf - mu).mean(-1, keepdims=True)
    y = (xf - mu) * lax.rsqrt(var + LN_EPS)
    if gain is not None:
        y = y * gain.astype(jnp.float32) + bias.astype(jnp.float32)
    return y.astype(x.dtype)


def _adaln(c, w_ada, b_ada):
    mod = jax.nn.silu(c) @ w_ada + b_ada
    return tuple(jnp.split(mod[:, None, :], 6, axis=-1))


def _modulate(x, shift, scale):
    return _layer_norm(x) * (1 + scale) + shift


def _masked_softmax(s, valid, sink):
    s = jnp.where(valid, s, NEG_INF)
    m = s.max(-1)
    if sink is not None:
        m = jnp.maximum(m, sink[..., None])
    p = jnp.exp(s - m[..., None])
    denom = p.sum(-1)
    if sink is not None:
        denom = denom + jnp.exp(sink[..., None] - m)
    return p, denom, m + jnp.log(denom)


def _banded_window_attention(q, k, v, span, step, slopes, sink):
    B, L, Hk, G, Dh = q.shape
    nb = -(-L // BLOCK)
    pad = nb * BLOCK - L
    qb = jnp.pad(q, ((0, 0), (0, pad), (0, 0), (0, 0), (0, 0))).reshape(B, nb, BLOCK, Hk, G, Dh)
    kp = jnp.pad(k, ((0, 0), (BLOCK, pad), (0, 0), (0, 0))).reshape(B, nb + 1, BLOCK, Hk, Dh)
    vp = jnp.pad(v, ((0, 0), (BLOCK, pad), (0, 0), (0, 0))).reshape(B, nb + 1, BLOCK, Hk, Dh)
    kband = jnp.concatenate([kp[:, :-1], kp[:, 1:]], axis=2)
    vband = jnp.concatenate([vp[:, :-1], vp[:, 1:]], axis=2)
    s = jnp.einsum('bnqhgd,bnkhd->bnhgqk', qb, kband, preferred_element_type=jnp.float32) * Dh ** -0.5
    rel = (jnp.arange(BLOCK)[:, None] + BLOCK) - jnp.arange(2 * BLOCK)[None, :]
    kpos = (jnp.arange(nb)[:, None] - 1) * BLOCK + jnp.arange(2 * BLOCK)[None, :]
    valid = (rel >= 0) & (rel <= span) & (kpos >= 0)[:, None, :]
    s = s - slopes[:, :, None, None] * (step * rel).astype(jnp.float32)
    p, denom, lse = _masked_softmax(s, valid[None, :, None, None], sink)
    o = jnp.einsum('bnhgqk,bnkhd->bnqhgd', p, vband.astype(jnp.float32))
    o = o / jnp.moveaxis(denom, -1, 2)[..., None]
    o = o.reshape(B, nb * BLOCK, Hk, G, Dh)[:, :L].astype(q.dtype)
    lse = jnp.moveaxis(lse, -1, 2).reshape(B, nb * BLOCK, Hk, G)[:, :L]
    return o, lse


def _gathered_window_attention(q, k_all, v_all, q_pos, k_start, n_keys, step, slopes, sink):
    Dh = q.shape[-1]
    dist = step * jnp.arange(n_keys)
    idx = q_pos[:, None] - dist[None, :] - k_start
    valid = idx >= 0
    idx = jnp.maximum(idx, 0)
    kg = jnp.take(k_all, idx, axis=1)
    vg = jnp.take(v_all, idx, axis=1)
    s = jnp.einsum('bqhgd,bqkhd->bhgqk', q, kg, preferred_element_type=jnp.float32) * Dh ** -0.5
    s = s - slopes[:, :, None, None] * dist.astype(jnp.float32)
    p, denom, lse = _masked_softmax(s, valid, sink)
    o = jnp.einsum('bhgqk,bqkhd->bqhgd', p, vg.astype(jnp.float32)) / jnp.moveaxis(denom, -1, 1)[..., None]
    return o.astype(q.dtype), jnp.moveaxis(lse, -1, 1)


def _to_residues(t, dil):
    B, S = t.shape[:2]
    rest = t.shape[2:]
    return t.reshape(B, S // dil, dil, *rest).swapaxes(1, 2).reshape(B * dil, S // dil, *rest)


def _from_residues(t, dil):
    Bd, L = t.shape[:2]
    rest = t.shape[2:]
    return t.reshape(Bd // dil, dil, L, *rest).swapaxes(1, 2).reshape(Bd // dil, L * dil, *rest)


def _combine_by_denominator(outs, lses):
    w = jax.nn.softmax(jnp.stack(lses), axis=0)
    o = jnp.einsum('gbsh,gbshd->bshd', w, jnp.stack(outs).astype(jnp.float32))
    return o.astype(outs[0].dtype)


def _prompt_attend(qa, ka, va, qb, kb, vb, sinks):
    S = qa.shape[1]
    o_swa, _ = _banded_window_attention(qa, ka, va, SWA_WINDOW, 1,
                                        _alibi_slopes(SWA_Q_HEADS).reshape(SWA_KV_HEADS, SWA_GROUP), sinks)
    keep = min(SWA_WINDOW, S)
    states = [jnp.stack([ka[:, S - keep:], va[:, S - keep:]], axis=2)]
    slopes = _alibi_slopes(N_DIL * DIL_HEADS).reshape(N_DIL, DIL_HEADS)
    outs, lses = [], []
    for g, (win, dil) in enumerate(DIL_PAIRS):
        qg = _to_residues(qb[:, :, g, :, None], dil)
        kg = _to_residues(kb[:, :, g], dil)
        vg = _to_residues(vb[:, :, g], dil)
        o, lse = _banded_window_attention(qg, kg, vg, win // dil, dil, slopes[g][:, None], None)
        outs.append(_from_residues(o[:, :, :, 0], dil))
        lses.append(_from_residues(lse[..., 0], dil))
        keep = min(win, S)
        states.append(jnp.stack([kb[:, S - keep:, g], vb[:, S - keep:, g]], axis=2))
    return o_swa, _combine_by_denominator(outs, lses), states


def _sample_window(q, cache_kv, k_new, v_new, win, dil, slopes, sink):
    L_buf = cache_kv.shape[1]
    n = q.shape[1]
    k_all = jnp.concatenate([cache_kv[:, :, 0], k_new], axis=1)
    v_all = jnp.concatenate([cache_kv[:, :, 1], v_new], axis=1)
    q_pos = PAST_LEN + jnp.arange(n)
    o, lse = _gathered_window_attention(q, k_all, v_all, q_pos, PAST_LEN - L_buf, win // dil + 1, dil, slopes, sink)
    keep = min(win, PAST_LEN + n)
    return o, lse, jnp.stack([k_all[:, -keep:], v_all[:, -keep:]], axis=2)


def _sample_attend(qa, ka, va, qb, kb, vb, sinks, cache_swa, cache_dils):
    o_swa, _, st = _sample_window(qa, cache_swa, ka, va, SWA_WINDOW, 1,
                                  _alibi_slopes(SWA_Q_HEADS).reshape(SWA_KV_HEADS, SWA_GROUP), sinks)
    states = [st]
    slopes = _alibi_slopes(N_DIL * DIL_HEADS).reshape(N_DIL, DIL_HEADS)
    outs, lses = [], []
    for g, (win, dil) in enumerate(DIL_PAIRS):
        o, lse, st = _sample_window(qb[:, :, g, :, None], cache_dils[g], kb[:, :, g], vb[:, :, g],
                                    win, dil, slopes[g][:, None], None)
        outs.append(o[:, :, :, 0])
        lses.append(lse[..., 0])
        states.append(st)
    return o_swa, _combine_by_denominator(outs, lses), states


def _project_in(h, w_in):
    B, S = h.shape[:2]
    z = h @ w_in
    sizes = (Q_A, KV_A, KV_A, QKV_B, QKV_B, QKV_B, D_MODEL, D_MODEL)
    offs, acc = [], 0
    for sz in sizes[:-1]:
        acc += sz
        offs.append(acc)
    qa, ka, va, qb, kb, vb, ga, gb = jnp.split(z, offs, axis=-1)
    qa = qa.reshape(B, S, SWA_KV_HEADS, SWA_GROUP, HEAD_DIM)
    ka = ka.reshape(B, S, SWA_KV_HEADS, HEAD_DIM)
    va = va.reshape(B, S, SWA_KV_HEADS, HEAD_DIM)
    qb = qb.reshape(B, S, N_DIL, DIL_HEADS, HEAD_DIM)
    kb = kb.reshape(B, S, N_DIL, DIL_HEADS, HEAD_DIM)
    vb = vb.reshape(B, S, N_DIL, DIL_HEADS, HEAD_DIM)
    return qa, ka, va, qb, kb, vb, ga, gb


def _merge(o_swa, o_dil, ga, gb, w_br_swa, w_br_dil, w_out):
    B, S = o_swa.shape[:2]
    y_a = o_swa.reshape(B, S, Q_A) @ w_br_swa
    y_b = o_dil.reshape(B, S, O_B) @ w_br_dil
    return (jax.nn.sigmoid(ga) * y_a + jax.nn.sigmoid(gb) * y_b) @ w_out


def _route(t, w_router, router_bias):
    T = t.shape[0]
    scores = jax.nn.sigmoid(jnp.matmul(t, w_router, preferred_element_type=jnp.float32))
    biased = scores + router_bias.astype(jnp.float32)
    group_score = lax.top_k(biased.reshape(T, N_EXPERT_GROUPS, EXPERTS_PER_GROUP), 2)[0].sum(-1)
    _, top_groups = lax.top_k(group_score, TOPK_GROUPS)
    group_mask = jax.nn.one_hot(top_groups, N_EXPERT_GROUPS, dtype=jnp.float32).sum(1) > 0
    expert_mask = jnp.repeat(group_mask, EXPERTS_PER_GROUP, axis=1)
    _, idx = lax.top_k(jnp.where(expert_mask, biased, -jnp.inf), TOP_K)
    w = jnp.take_along_axis(scores, idx, axis=1)
    return idx, w / w.sum(-1, keepdims=True) * ROUTED_SCALE


def _routed_experts(h, idx, wts, w_gate, w_up, w_down):
    T, D = h.shape
    A = T * TOP_K
    e_flat = idx.reshape(A)
    tok_flat = jnp.arange(A, dtype=jnp.int32) // TOP_K
    order = jnp.argsort(e_flat)
    e_sorted = e_flat[order]
    counts = jnp.zeros((N_EXPERTS,), jnp.int32).at[e_flat].add(1)
    starts = jnp.cumsum(counts) - counts
    padded = (counts + MOE_BLOCK - 1) // MOE_BLOCK * MOE_BLOCK
    pends = jnp.cumsum(padded)
    dest = (pends - padded)[e_sorted] + jnp.arange(A) - starts[e_sorted]
    n_blocks = -(-A // MOE_BLOCK) + N_EXPERTS
    P = n_blocks * MOE_BLOCK
    slot_tok = jnp.full((P,), T, jnp.int32).at[dest].set(tok_flat[order])
    slot_w = jnp.zeros((P,), jnp.float32).at[dest].set(wts.reshape(A)[order])
    block_e = jnp.minimum(jnp.searchsorted(pends, jnp.arange(n_blocks) * MOE_BLOCK, side='right'), N_EXPERTS - 1)
    h_pad = jnp.concatenate([h, jnp.zeros((1, D), h.dtype)], axis=0)

    def run_block(args):
        tok, e, wb = args
        xb = h_pad[tok]
        a = jax.nn.silu(xb @ w_gate[e]) * (xb @ w_up[e])
        return (a @ w_down[e]) * wb[:, None]

    out = lax.map(run_block, (slot_tok.reshape(n_blocks, MOE_BLOCK), block_e,
                              slot_w.reshape(n_blocks, MOE_BLOCK).astype(h.dtype)))
    return jax.ops.segment_sum(out.reshape(P, D), slot_tok, num_segments=T + 1)[:T]


def _moe(h, w_router, router_bias, w_exp_gate, w_exp_up, w_exp_down, w_sh_gate, w_sh_up, w_sh_down):
    B, S, D = h.shape
    t = h.reshape(B * S, D)
    idx, wts = _route(t, w_router, router_bias)
    y = _routed_experts(t, idx, wts, w_exp_gate, w_exp_up, w_exp_down)
    y = y + (jax.nn.silu(t @ w_sh_gate) * (t @ w_sh_up)) @ w_sh_down
    return y.reshape(B, S, D)


def _trunk_layer(x, c, attend, w_ada, b_ada, w_in, w_br_swa, w_br_dil, w_out, ln1_g, ln1_b,
                 w_router, router_bias, w_exp_gate, w_exp_up, w_exp_down, w_sh_gate, w_sh_up, w_sh_down,
                 ln2_g, ln2_b):
    shift1, scale1, gate1, shift2, scale2, gate2 = _adaln(c, w_ada, b_ada)
    qa, ka, va, qb, kb, vb, ga, gb = _project_in(_modulate(x, shift1, scale1), w_in)
    o_swa, o_dil, states = attend(qa, ka, va, qb, kb, vb)
    mix = _merge(o_swa, o_dil, ga, gb, w_br_swa, w_br_dil, w_out)
    x = _layer_norm(ALPHA * x + gate1 * mix, ln1_g, ln1_b)
    ffn = _moe(_modulate(x, shift2, scale2), w_router, router_bias, w_exp_gate, w_exp_up, w_exp_down,
               w_sh_gate, w_sh_up, w_sh_down)
    x = _layer_norm(ALPHA * x + gate2 * ffn, ln2_g, ln2_b)
    return x, states


def setup_inputs(seed: int = 0) -> dict:
    key = jax.random.key(seed)
    ks = jax.random.split(key, 27)
    f32 = jnp.float32

    def nrm(k, shape, scale):
        return jax.random.normal(k, shape, f32) * scale

    def kv_buf(k, win, heads):
        return nrm(k, (DEPTH, DEC_BATCH, min(win, PAST_LEN), 2, heads, HEAD_DIM), 1.0)

    v_scale = jnp.concatenate([jnp.ones((Q_A + KV_A,), f32), jnp.full((KV_A,), BETA, f32),
                               jnp.ones((2 * QKV_B,), f32), jnp.full((QKV_B,), BETA, f32),
                               jnp.ones((2 * D_MODEL,), f32)])
    return {
        'x_prompt': nrm(ks[0], (BATCH, SEQ, D_MODEL), 1.0),
        'x_sample': nrm(ks[1], (DEC_BATCH, DEC_SEQ, D_MODEL), 1.0),
        'cache_swa_kv': kv_buf(ks[2], SWA_WINDOW, SWA_KV_HEADS),
        'cache_dil1_kv': kv_buf(ks[3], DIL_PAIRS[0][0], DIL_HEADS),
        'cache_dil2_kv': kv_buf(ks[4], DIL_PAIRS[1][0], DIL_HEADS),
        'cache_dil3_kv': kv_buf(ks[5], DIL_PAIRS[2][0], DIL_HEADS),
        'c_prompt': nrm(ks[6], (BATCH, D_MODEL), 1.0),
        'c_sample': nrm(ks[7], (DEC_BATCH, D_MODEL), 1.0),
        'w_ada': nrm(ks[8], (DEPTH, D_MODEL, 6 * D_MODEL), 0.5 * D_MODEL ** -0.5),
        'b_ada': nrm(ks[9], (DEPTH, 6 * D_MODEL), 0.02),
        'w_in': nrm(ks[10], (DEPTH, D_MODEL, N_IN), D_MODEL ** -0.5) * v_scale,
        'attn_sinks': nrm(ks[11], (DEPTH, SWA_Q_HEADS), 0.5),
        'w_br_swa': nrm(ks[12], (DEPTH, Q_A, D_MODEL), Q_A ** -0.5),
        'w_br_dil': nrm(ks[13], (DEPTH, O_B, D_MODEL), O_B ** -0.5),
        'w_out': nrm(ks[14], (DEPTH, D_MODEL, D_MODEL), BETA * D_MODEL ** -0.5),
        'ln1_g': 1.0 + nrm(ks[15], (DEPTH, D_MODEL), 0.02),
        'ln1_b': nrm(ks[16], (DEPTH, D_MODEL), 0.02),
        'w_router': nrm(ks[17], (DEPTH, D_MODEL, N_EXPERTS), D_MODEL ** -0.5),
        'router_bias': nrm(ks[18], (DEPTH, N_EXPERTS), 0.01),
        'w_exp_gate': nrm(ks[19], (DEPTH, N_EXPERTS, D_MODEL, D_EXPERT), D_MODEL ** -0.5),
        'w_exp_up': nrm(ks[20], (DEPTH, N_EXPERTS, D_MODEL, D_EXPERT), D_MODEL ** -0.5),
        'w_exp_down': nrm(ks[21], (DEPTH, N_EXPERTS, D_EXPERT, D_MODEL), BETA * D_EXPERT ** -0.5),
        'w_sh_gate': nrm(ks[22], (DEPTH, D_MODEL, D_SHARED), D_MODEL ** -0.5),
        'w_sh_up': nrm(ks[23], (DEPTH, D_MODEL, D_SHARED), D_MODEL ** -0.5),
        'w_sh_down': nrm(ks[24], (DEPTH, D_SHARED, D_MODEL), BETA * D_SHARED ** -0.5),
        'ln2_g': 1.0 + nrm(ks[25], (DEPTH, D_MODEL), 0.02),
        'ln2_b': nrm(ks[26], (DEPTH, D_MODEL), 0.02),
    }


def reference(x_prompt, x_sample, cache_swa_kv, cache_dil1_kv, cache_dil2_kv, cache_dil3_kv, c_prompt, c_sample,
              w_ada, b_ada, w_in, attn_sinks, w_br_swa, w_br_dil, w_out, ln1_g, ln1_b, w_router, router_bias,
              w_exp_gate, w_exp_up, w_exp_down, w_sh_gate, w_sh_up, w_sh_down, ln2_g, ln2_b):
    xp, xs = x_prompt, x_sample
    p_states, s_states = [], []
    for l in range(DEPTH):
        sinks = attn_sinks[l].reshape(SWA_KV_HEADS, SWA_GROUP).astype(jnp.float32)
        w_l = (w_ada[l], b_ada[l], w_in[l], w_br_swa[l], w_br_dil[l], w_out[l], ln1_g[l], ln1_b[l],
               w_router[l], router_bias[l], w_exp_gate[l], w_exp_up[l], w_exp_down[l],
               w_sh_gate[l], w_sh_up[l], w_sh_down[l], ln2_g[l], ln2_b[l])
        xp, st_p = _trunk_layer(xp, c_prompt, functools.partial(_prompt_attend, sinks=sinks), *w_l)
        attend_s = functools.partial(_sample_attend, sinks=sinks, cache_swa=cache_swa_kv[l],
                                     cache_dils=(cache_dil1_kv[l], cache_dil2_kv[l], cache_dil3_kv[l]))
        xs, st_s = _trunk_layer(xs, c_sample, attend_s, *w_l)
        p_states.append(st_p)
        s_states.append(st_s)
    swa_p = jnp.stack([s[0] for s in p_states])
    dil1_p = jnp.stack([s[1] for s in p_states])
    dil2_p = jnp.stack([s[2] for s in p_states])
    dil3_p = jnp.stack([s[3] for s in p_states])
    swa_s = jnp.stack([s[0] for s in s_states])
    dil1_s = jnp.stack([s[1] for s in s_states])
    dil2_s = jnp.stack([s[2] for s in s_states])
    dil3_s = jnp.stack([s[3] for s in s_states])
    return (xp, xs, swa_p, dil1_p, dil2_p, dil3_p, swa_s, dil1_s, dil2_s, dil3_s)
```

```python
import functools

import jax
import jax.numpy as jnp
import numpy as np
from jax import lax
from jax.experimental import pallas as pl
from jax.experimental.pallas import tpu as pltpu

F32 = jnp.float32
BF16 = jnp.bfloat16

D_MODEL = 1024
HEAD_DIM = 64
SWA_WINDOW = 128
SWA_Q_HEADS = 8
SWA_KV_HEADS = 2
SWA_GROUP = SWA_Q_HEADS // SWA_KV_HEADS
DIL_PAIRS = ((128, 1), (512, 4), (2048, 16))
N_DIL = len(DIL_PAIRS)
DIL_HEADS = 4
BLOCK = 128
Q_A = SWA_Q_HEADS * HEAD_DIM
KV_A = SWA_KV_HEADS * HEAD_DIM
QKV_B = N_DIL * DIL_HEADS * HEAD_DIM
O_B = DIL_HEADS * HEAD_DIM
C_A = Q_A + 2 * KV_A
C_B = 3 * O_B
N_QKV = C_A + N_DIL * C_B
N_IN = N_QKV + 2 * D_MODEL
N_EXPERTS = 64
TOP_K = 8
N_EXPERT_GROUPS = 8
EXPERTS_PER_GROUP = N_EXPERTS // N_EXPERT_GROUPS
TOPK_GROUPS = 4
D_EXPERT = 256
D_SHARED = 256
ROUTED_SCALE = 2.5
LN_EPS = 1e-5
NEG_INF = -1e30
PAST_LEN = 16384
DEPTH = 1
ALPHA = (2.0 * DEPTH) ** 0.25

LANES = 128
SUBLANES = 8
VMEM_LIMIT_BYTES = 56 * 1024 * 1024
MOE_CHUNK = 128
ROW_STRIDE = MOE_CHUNK + SUBLANES

SWA_SLOPES = tuple(float(v) for v in np.asarray(2.0 ** (-8.0 * np.arange(1, SWA_Q_HEADS + 1) / SWA_Q_HEADS), np.float32))
DIL_SLOPES = tuple(float(v) for v in np.asarray(
    2.0 ** (-8.0 * np.arange(1, N_DIL * DIL_HEADS + 1) / (N_DIL * DIL_HEADS)), np.float32))


def _cparams(*sem):
    return pltpu.CompilerParams(dimension_semantics=sem, vmem_limit_bytes=VMEM_LIMIT_BYTES)


def _ln(x):
    mu = jnp.mean(x, axis=-1, keepdims=True)
    xc = x - mu
    var = jnp.mean(xc * xc, axis=-1, keepdims=True)
    return xc * lax.rsqrt(var + LN_EPS)


def _silu(x):
    return x * jax.nn.sigmoid(x)


def _dot(a, b):
    return jnp.dot(a, b, preferred_element_type=F32)


def _dot_nt(a, b):
    return lax.dot_general(a, b, (((1,), (1,)), ((), ())), preferred_element_type=F32)


def _ada_body(c_ref, w_ref, b_ref, o_ref):
    s = _silu(c_ref[...]).astype(BF16)
    o_ref[...] = _dot(s, w_ref[...].astype(BF16)) + b_ref[...]


def _ada(c, w_ada, b_ada):
    rows, d = c.shape
    n = w_ada.shape[1]
    tn = 1024
    return pl.pallas_call(
        _ada_body,
        out_shape=jax.ShapeDtypeStruct((rows, n), F32),
        grid=(n // tn,),
        in_specs=[pl.BlockSpec((rows, d), lambda j: (0, 0)),
                  pl.BlockSpec((d, tn), lambda j: (0, j)),
                  pl.BlockSpec((1, tn), lambda j: (0, j))],
        out_specs=pl.BlockSpec((rows, tn), lambda j: (0, j)),
        compiler_params=_cparams("arbitrary"),
        name="ada_mod",
    )(c, w_ada, b_ada.reshape(1, n))


def _inproj_body(x_ref, sh_ref, sc_ref, w_ref, qa_ref, qb0_ref, qb1_ref, qb2_ref, g_ref,
                 s2_ref, s3_ref, s4_ref, s5_ref, *, tm):
    j = pl.program_id(1)
    last = pl.num_programs(1) - 1
    h = (_ln(x_ref[...]) * (1.0 + sc_ref[...]) + sh_ref[...]).astype(BF16)
    za = _dot(h, w_ref[:, 0:C_A])
    qa_ref[...] = za.astype(BF16)
    zb = []
    for g, ref in enumerate((qb0_ref, qb1_ref, qb2_ref)):
        z = _dot(h, w_ref[:, C_A + g * C_B:C_A + (g + 1) * C_B])
        ref[...] = z.astype(BF16)
        zb.append(z)
    gw = 512
    for c in range(2 * D_MODEL // gw):
        z = _dot(h, w_ref[:, N_QKV + c * gw:N_QKV + (c + 1) * gw])
        g_ref[:, c * gw:(c + 1) * gw] = jax.nn.sigmoid(z).astype(BF16)
    s5_ref[...] = zb[2][:, O_B:].T

    @pl.when(j == last)
    def _():
        s4_ref[...] = zb[1][:, O_B:].T
        s3_ref[...] = zb[0][tm - BLOCK:, O_B:].T
        s2_ref[...] = za[tm - BLOCK:, Q_A:].T


def _inproj_prompt(x, shift, scale, w_in_p):
    B, S, D = x.shape
    tm = 512
    nj = S // tm
    assert S % tm == 0 and DIL_PAIRS[1][0] == tm and DIL_PAIRS[2][0] == S and SWA_WINDOW == BLOCK
    row = lambda b, j: (b, 0, 0)
    tok = lambda b, j: (b, j, 0)
    outs = pl.pallas_call(
        functools.partial(_inproj_body, tm=tm),
        out_shape=(jax.ShapeDtypeStruct((B, S, C_A), BF16),
                   jax.ShapeDtypeStruct((B, S, C_B), BF16),
                   jax.ShapeDtypeStruct((B, S, C_B), BF16),
                   jax.ShapeDtypeStruct((B, S, C_B), BF16),
                   jax.ShapeDtypeStruct((B, S, 2 * D), BF16),
                   jax.ShapeDtypeStruct((B, 2 * KV_A, BLOCK), F32),
                   jax.ShapeDtypeStruct((B, 2 * O_B, BLOCK), F32),
                   jax.ShapeDtypeStruct((B, 2 * O_B, tm), F32),
                   jax.ShapeDtypeStruct((B, 2 * O_B, S), F32)),
        grid=(B, nj),
        in_specs=[pl.BlockSpec((None, tm, D), tok),
                  pl.BlockSpec((None, 1, D), row),
                  pl.BlockSpec((None, 1, D), row),
                  pl.BlockSpec((D, N_IN), lambda b, j: (0, 0))],
        out_specs=(pl.BlockSpec((None, tm, C_A), tok),
                   pl.BlockSpec((None, tm, C_B), tok),
                   pl.BlockSpec((None, tm, C_B), tok),
                   pl.BlockSpec((None, tm, C_B), tok),
                   pl.BlockSpec((None, tm, 2 * D), tok),
                   pl.BlockSpec((None, 2 * KV_A, BLOCK), row),
                   pl.BlockSpec((None, 2 * O_B, BLOCK), row),
                   pl.BlockSpec((None, 2 * O_B, tm), row),
                   pl.BlockSpec((None, 2 * O_B, tm), lambda b, j: (b, 0, j))),
        compiler_params=_cparams("arbitrary", "arbitrary"),
        name="inproj_prompt",
    )(x, shift, scale, w_in_p)
    return outs


def _band_attn_body(sink_ref, cur_ref, prev_ref, o_ref, *lse_refs, n_kv, group, step, slopes, use_sink):
    i = pl.program_id(2)
    nq = n_kv * group
    cur = cur_ref[...]
    prev = prev_ref[...]
    qi = lax.broadcasted_iota(jnp.int32, (BLOCK, 2 * BLOCK), 0)
    kj = lax.broadcasted_iota(jnp.int32, (BLOCK, 2 * BLOCK), 1)
    rel = qi + BLOCK - kj
    valid = (rel >= 0) & (rel <= BLOCK) & ((kj >= BLOCK) | (i > 0))
    dist = (rel * step).astype(F32)
    k_off = nq * HEAD_DIM
    v_off = k_off + n_kv * HEAD_DIM
    outs, lses = [], []
    for hk in range(n_kv):
        ks = slice(k_off + hk * HEAD_DIM, k_off + (hk + 1) * HEAD_DIM)
        vs = slice(v_off + hk * HEAD_DIM, v_off + (hk + 1) * HEAD_DIM)
        k = jnp.concatenate([prev[:, ks], cur[:, ks]], axis=0)
        v = jnp.concatenate([prev[:, vs], cur[:, vs]], axis=0)
        for g in range(group):
            h = hk * group + g
            q = cur[:, h * HEAD_DIM:(h + 1) * HEAD_DIM]
            s = _dot_nt(q, k) - slopes[h] * dist
            s = jnp.where(valid, s, NEG_INF)
            m = jnp.max(s, axis=-1, keepdims=True)
            if use_sink:
                sink = sink_ref[h]
                m = jnp.maximum(m, sink)
            p = jnp.exp(s - m)
            denom = jnp.sum(p, axis=-1, keepdims=True)
            if use_sink:
                denom = denom + jnp.exp(sink - m)
            outs.append(_dot(p.astype(BF16), v) / denom)
            lses.append(m + jnp.log(denom))
    o_ref[...] = jnp.concatenate(outs, axis=-1).astype(BF16)
    if lse_refs:
        lse_refs[0][...] = jnp.concatenate(
            [jnp.broadcast_to(l, (BLOCK, HEAD_DIM)) for l in lses], axis=-1)


def _band_attn(qkv, sinks, *, n_kv, group, dil, slopes, use_sink, emit_lse):
    B, S, C = qkv.shape
    L = S // dil
    nb = L // BLOCK
    nq = n_kv * group
    ow = nq * HEAD_DIM
    view = qkv.reshape(B, L, dil * C)
    out_shape = [jax.ShapeDtypeStruct((B, L, dil * ow), BF16)]
    out_specs = [pl.BlockSpec((None, BLOCK, ow), lambda b, r, i, *_: (b, i, r))]
    if emit_lse:
        out_shape.append(jax.ShapeDtypeStruct((B, L, dil * ow), F32))
        out_specs.append(pl.BlockSpec((None, BLOCK, ow), lambda b, r, i, *_: (b, i, r)))
    res = pl.pallas_call(
        functools.partial(_band_attn_body, n_kv=n_kv, group=group, step=dil, slopes=slopes, use_sink=use_sink),
        out_shape=tuple(out_shape),
        grid_spec=pltpu.PrefetchScalarGridSpec(
            num_scalar_prefetch=1,
            grid=(B, dil, nb),
            in_specs=[pl.BlockSpec((None, BLOCK, C), lambda b, r, i, *_: (b, i, r)),
                      pl.BlockSpec((None, BLOCK, C), lambda b, r, i, *_: (b, jnp.maximum(i - 1, 0), r))],
            out_specs=tuple(out_specs)),
        compiler_params=_cparams("arbitrary", "arbitrary", "arbitrary"),
        name=f"band_attn_d{dil}",
    )(sinks, view, view)
    return tuple(r.reshape(B, S, ow) for r in res)


def _split_bf16(x):
    hi = x.astype(BF16)
    lo = (x - hi.astype(F32)).astype(BF16)
    return hi, lo


def _merge_body(*refs, n_groups):
    x_ref, oswa_ref = refs[0], refs[1]
    od_refs = refs[2:2 + n_groups]
    lse_refs = refs[2 + n_groups:2 + 2 * n_groups] if n_groups > 1 else ()
    k = 2 + (2 * n_groups if n_groups > 1 else 1)
    (sg_ref, g1_ref, sh2_ref, sc2_ref, wa_ref, wb_ref, wo_ref, lg_ref, lb_ref, wrh_ref, wrl_ref,
     x1_ref, h2_ref, lt_ref) = refs[k:]
    if n_groups > 1:
        ls = [r[...] for r in lse_refs]
        m = functools.reduce(jnp.maximum, ls)
        es = [jnp.exp(l - m) for l in ls]
        num = sum(e * r[...].astype(F32) for e, r in zip(es, od_refs))
        odil = num / sum(es)
    else:
        odil = od_refs[0][...].astype(F32)
    ya = _dot(oswa_ref[...].astype(BF16), wa_ref[...])
    yb = _dot(odil.astype(BF16), wb_ref[...])
    d = ya.shape[-1]
    u = sg_ref[:, :d].astype(F32) * ya + sg_ref[:, d:].astype(F32) * yb
    mix = _dot(u.astype(BF16), wo_ref[...])
    x1 = _ln(ALPHA * x_ref[...] + g1_ref[...] * mix) * lg_ref[...] + lb_ref[...]
    x1_ref[...] = x1
    h2 = _ln(x1) * (1.0 + sc2_ref[...]) + sh2_ref[...]
    hi, lo = _split_bf16(h2)
    h2_ref[...] = hi
    wrh = wrh_ref[...]
    lt_ref[...] = _dot_nt(wrh, hi) + _dot_nt(wrh, lo) + _dot_nt(wrl_ref[...], hi)


def _merge(x, oswa, ods, lses, sg, g1, sh2, sc2, wts, *, tm, per_token_mod):
    B, S, D = x.shape
    nj = S // tm
    n_groups = len(ods)
    wa, wb, wo, lg, lb, wrh, wrl = wts
    tok = lambda b, j: (b, j, 0)
    mod = (lambda b, j: (b, j, 0)) if per_token_mod else (lambda b, j: (b, 0, 0))
    mod_rows = tm if per_token_mod else 1
    const = lambda b, j: (0, 0)
    in_specs = [pl.BlockSpec((None, tm, D), tok), pl.BlockSpec((None, tm, Q_A), tok)]
    in_specs += [pl.BlockSpec((None, tm, O_B), tok)] * n_groups
    if n_groups > 1:
        in_specs += [pl.BlockSpec((None, tm, O_B), tok)] * n_groups
    in_specs += [pl.BlockSpec((None, tm, 2 * D), tok)]
    in_specs += [pl.BlockSpec((None, mod_rows, D), mod)] * 3
    in_specs += [pl.BlockSpec(w.shape, const) for w in wts]
    args = [x, oswa, *ods, *(lses if n_groups > 1 else ()), sg, g1, sh2, sc2, *wts]
    return pl.pallas_call(
        functools.partial(_merge_body, n_groups=n_groups),
        out_shape=(jax.ShapeDtypeStruct((B, S, D), F32),
                   jax.ShapeDtypeStruct((B, S, D), BF16),
                   jax.ShapeDtypeStruct((N_EXPERTS, B * S), F32)),
        grid=(B, nj),
        in_specs=in_specs,
        out_specs=(pl.BlockSpec((None, tm, D), tok),
                   pl.BlockSpec((None, tm, D), tok),
                   pl.BlockSpec((N_EXPERTS, tm), lambda b, j: (0, b * nj + j))),
        compiler_params=_cparams("arbitrary", "arbitrary"),
        name="merge",
    )(*args)


def _route_body(lt_ref, bias_ref, dest_ref, wk_ref, cnt_ref, cst_ref, *, tw):
    G = EXPERTS_PER_GROUP
    sub = lax.broadcasted_iota(jnp.int32, (G, tw), 0)
    ninf = -jnp.inf
    scores, biased = [], []
    for g in range(N_EXPERT_GROUPS):
        s = jax.nn.sigmoid(lt_ref[g * G:(g + 1) * G, :])
        scores.append(s)
        biased.append(s + bias_ref[g * G:(g + 1) * G, :])
    gs = []
    for g in range(N_EXPERT_GROUPS):
        m1 = jnp.max(biased[g], axis=0, keepdims=True)
        i1 = jnp.min(jnp.where(biased[g] == m1, sub, G), axis=0, keepdims=True)
        m2 = jnp.max(jnp.where(sub == i1, ninf, biased[g]), axis=0, keepdims=True)
        gs.append(m1 + m2)
    gsel = [jnp.zeros((1, tw), jnp.bool_) for _ in range(N_EXPERT_GROUPS)]
    for _ in range(TOPK_GROUPS):
        mx = functools.reduce(jnp.maximum, gs)
        ix = functools.reduce(jnp.minimum, [jnp.where(gs[g] == mx, g, N_EXPERT_GROUPS) for g in range(N_EXPERT_GROUPS)])
        for g in range(N_EXPERT_GROUPS):
            pick = ix == g
            gsel[g] = gsel[g] | pick
            gs[g] = jnp.where(pick, ninf, gs[g])
    work = [jnp.where(gsel[g], biased[g], ninf) for g in range(N_EXPERT_GROUPS)]
    chosen = [jnp.zeros((G, tw), jnp.bool_) for _ in range(N_EXPERT_GROUPS)]
    idxs = []
    for _ in range(TOP_K):
        mx = functools.reduce(jnp.maximum, [jnp.max(w, axis=0, keepdims=True) for w in work])
        ix = functools.reduce(jnp.minimum, [
            jnp.min(jnp.where(work[g] == mx, sub + g * G, N_EXPERTS), axis=0, keepdims=True)
            for g in range(N_EXPERT_GROUPS)])
        idxs.append(ix)
        for g in range(N_EXPERT_GROUPS):
            pick = (sub + g * G) == ix
            chosen[g] = chosen[g] | pick
            work[g] = jnp.where(pick, ninf, work[g])
    wsel = [jnp.where(chosen[g], scores[g], 0.0) for g in range(N_EXPERT_GROUPS)]
    wsum = sum(jnp.sum(w, axis=0, keepdims=True) for w in wsel)
    wn = [w / wsum * ROUTED_SCALE for w in wsel]
    cm = jnp.concatenate([c.astype(F32) for c in chosen], axis=0)
    r_i = lax.broadcasted_iota(jnp.int32, (LANES, LANES), 0)
    c_i = lax.broadcasted_iota(jnp.int32, (LANES, LANES), 1)
    upper = (r_i < c_i).astype(BF16)
    carry = jnp.zeros((N_EXPERTS, 1), F32)
    ranks = []
    for c in range(tw // LANES):
        blk = cm[:, c * LANES:(c + 1) * LANES]
        ranks.append(_dot(blk.astype(BF16), upper) + carry)
        carry = carry + jnp.sum(blk, axis=1, keepdims=True)
    rank = jnp.concatenate(ranks, axis=1)
    nchunk = jnp.floor((carry + (MOE_CHUNK - 1)) * (1.0 / MOE_CHUNK))
    e_r = lax.broadcasted_iota(jnp.int32, (N_EXPERTS, N_EXPERTS), 0)
    e_c = lax.broadcasted_iota(jnp.int32, (N_EXPERTS, N_EXPERTS), 1)
    lower = (e_c < e_r).astype(BF16)
    cstart = _dot(lower, jnp.broadcast_to(nchunk, (N_EXPERTS, LANES)).astype(BF16))
    slot = cstart[:, :1] * float(MOE_CHUNK) + rank
    dests, wks = [], []
    for k in range(TOP_K):
        d = jnp.zeros((1, tw), F32)
        w = jnp.zeros((1, tw), F32)
        for g in range(N_EXPERT_GROUPS):
            pick = (sub + g * G) == idxs[k]
            d = d + jnp.sum(jnp.where(pick, slot[g * G:(g + 1) * G, :], 0.0), axis=0, keepdims=True)
            w = w + jnp.sum(jnp.where(pick, wn[g], 0.0), axis=0, keepdims=True)
        dests.append(d)
        wks.append(w)
    dest_ref[...] = jnp.concatenate(dests, axis=0).astype(jnp.int32)
    wk_ref[...] = jnp.concatenate(wks, axis=0)
    cnt_ref[...] = jnp.broadcast_to(carry, (N_EXPERTS, LANES)).astype(jnp.int32)
    cst_ref[...] = cstart.astype(jnp.int32)


def _route(lt, bias_col, *, tw):
    E, T = lt.shape
    nt = T // tw
    assert tw // MOE_CHUNK + N_EXPERTS <= 256
    return pl.pallas_call(
        functools.partial(_route_body, tw=tw),
        out_shape=(jax.ShapeDtypeStruct((TOP_K, T), jnp.int32),
                   jax.ShapeDtypeStruct((TOP_K, T), F32),
                   jax.ShapeDtypeStruct((nt, E, LANES), jnp.int32),
                   jax.ShapeDtypeStruct((nt, E, LANES), jnp.int32)),
        grid=(nt,),
        in_specs=[pl.BlockSpec((E, tw), lambda i: (0, i)),
                  pl.BlockSpec((E, 1), lambda i: (0, 0))],
        out_specs=(pl.BlockSpec((TOP_K, tw), lambda i: (0, i)),
                   pl.BlockSpec((TOP_K, tw), lambda i: (0, i)),
                   pl.BlockSpec((None, E, LANES), lambda i: (i, 0, 0)),
                   pl.BlockSpec((None, E, LANES), lambda i: (i, 0, 0))),
        compiler_params=_cparams("arbitrary"),
        name="route",
    )(lt, bias_col)


def _moe_body(cnt_ref, cst_ref, tok_ref, w_ref, h_ref, wgu_ref, wd_ref, o_ref, src_ref, acc_ref, xt_ref, yt_ref,
              *, tm):
    i = pl.program_id(0)
    e = pl.program_id(1)
    nchunks_lane = D_MODEL // LANES

    @pl.when(e == 0)
    def _():
        def relayout(a, c):
            rows = h_ref[pl.ds(pl.multiple_of(a * 16, 16), 16), :].astype(F32)
            for j in range(nchunks_lane):
                src_ref[pl.ds(a * (16 * SUBLANES) + j, 16, stride=SUBLANES), :] = rows[:, j * LANES:(j + 1) * LANES]
            return c
        lax.fori_loop(0, tm // 16, relayout, 0)
        src_ref[pl.ds(tm * SUBLANES, SUBLANES), :] = jnp.zeros((SUBLANES, LANES), F32)
        acc_ref[...] = jnp.zeros(acc_ref.shape, F32)

    n = cnt_ref[i, e]
    c0 = cst_ref[i, e]
    eye = (lax.broadcasted_iota(jnp.int32, (MOE_CHUNK, MOE_CHUNK), 0)
           == lax.broadcasted_iota(jnp.int32, (MOE_CHUNK, MOE_CHUNK), 1))

    def chunk(c, carry):
        base = (c0 + c) * MOE_CHUNK
        for mi in range(MOE_CHUNK):
            t8 = pl.multiple_of(tok_ref[0, 0, base + mi] * SUBLANES, SUBLANES)
            xt_ref[pl.ds(mi, SUBLANES, stride=ROW_STRIDE), :] = src_ref[pl.ds(t8, SUBLANES), :]
        x = jnp.concatenate([xt_ref[pl.ds(j * ROW_STRIDE, MOE_CHUNK), :] for j in range(nchunks_lane)],
                            axis=-1).astype(BF16)
        gu = _dot(x, wgu_ref[...])
        a = _silu(gu[:, :D_EXPERT]) * gu[:, D_EXPERT:]
        wrow = w_ref[pl.ds(c0 + c, 1), :]
        wcol = jnp.sum(jnp.where(eye, wrow, 0.0), axis=1, keepdims=True)
        y = _dot(a.astype(BF16), wd_ref[...]) * wcol
        for j in range(nchunks_lane):
            yt_ref[pl.ds(j * ROW_STRIDE, MOE_CHUNK), :] = y[:, j * LANES:(j + 1) * LANES]
        batch = 8
        for b in range(MOE_CHUNK // batch):
            vals = []
            for u in range(batch):
                mi = b * batch + u
                t8 = pl.multiple_of(tok_ref[0, 0, base + mi] * SUBLANES, SUBLANES)
                vals.append((t8, acc_ref[pl.ds(t8, SUBLANES), :] + yt_ref[pl.ds(mi, SUBLANES, stride=ROW_STRIDE), :]))
            for t8, v in vals:
                acc_ref[pl.ds(t8, SUBLANES), :] = v
        return carry

    lax.fori_loop(0, (n + (MOE_CHUNK - 1)) // MOE_CHUNK, chunk, 0)

    @pl.when(e == pl.num_programs(1) - 1)
    def _():
        def writeback(a, c):
            for j in range(nchunks_lane):
                o_ref[pl.ds(pl.multiple_of(a * SUBLANES, SUBLANES), SUBLANES), j * LANES:(j + 1) * LANES] = (
                    acc_ref[pl.ds(a * (SUBLANES * SUBLANES) + j, SUBLANES, stride=SUBLANES), :])
            return c
        lax.fori_loop(0, tm // SUBLANES, writeback, 0)


def _moe(h2, cnt, cst, slot_tok, slot_w, wgu, wd, *, tm):
    T, D = h2.shape
    nt = T // tm
    n_slots = slot_tok.shape[-1]
    stage_rows = SUBLANES * ROW_STRIDE
    return pl.pallas_call(
        functools.partial(_moe_body, tm=tm),
        out_shape=jax.ShapeDtypeStruct((T, D), F32),
        grid_spec=pltpu.PrefetchScalarGridSpec(
            num_scalar_prefetch=2,
            grid=(nt, N_EXPERTS),
            in_specs=[pl.BlockSpec((1, 1, n_slots), lambda i, e, *_: (i, 0, 0), memory_space=pltpu.SMEM),
                      pl.BlockSpec((None, n_slots // MOE_CHUNK, MOE_CHUNK), lambda i, e, *_: (i, 0, 0)),
                      pl.BlockSpec((tm, D), lambda i, e, *_: (i, 0)),
                      pl.BlockSpec((None, D, 2 * D_EXPERT), lambda i, e, *_: (e, 0, 0)),
                      pl.BlockSpec((None, D_EXPERT, D), lambda i, e, *_: (e, 0, 0))],
            out_specs=pl.BlockSpec((tm, D), lambda i, e, *_: (i, 0)),
            scratch_shapes=[pltpu.VMEM(((tm + 1) * SUBLANES, LANES), F32),
                            pltpu.VMEM(((tm + 1) * SUBLANES, LANES), F32),
                            pltpu.VMEM((stage_rows, LANES), F32),
                            pltpu.VMEM((stage_rows, LANES), F32)]),
        compiler_params=_cparams("arbitrary", "arbitrary"),
        name="moe_routed",
    )(cnt, cst, slot_tok, slot_w, h2, wgu, wd)


def _slot_lists(dest, wk, *, tw, n_slots):
    K, T = dest.shape
    nt = T // tw
    t = jnp.arange(T, dtype=jnp.int32)
    flat = (t // tw)[None, :] * n_slots + dest
    tok = jnp.full((nt * n_slots,), tw, jnp.int32).at[flat.reshape(-1)].set(
        jnp.broadcast_to(t % tw, (K, T)).reshape(-1), unique_indices=True)
    w = jnp.zeros((nt * n_slots,), F32).at[flat.reshape(-1)].set(wk.reshape(-1), unique_indices=True)
    return tok.reshape(nt, 1, n_slots), w.reshape(nt, n_slots // MOE_CHUNK, MOE_CHUNK)


def _final_body(x1_ref, h2_ref, r_ref, g2_ref, wsgu_ref, wsd_ref, lg_ref, lb_ref, o_ref):
    gu = _dot(h2_ref[...], wsgu_ref[...])
    a = _silu(gu[:, :D_SHARED]) * gu[:, D_SHARED:]
    ffn = r_ref[...] + _dot(a.astype(BF16), wsd_ref[...])
    o_ref[...] = _ln(ALPHA * x1_ref[...] + g2_ref[...] * ffn) * lg_ref[...] + lb_ref[...]


def _final(x1, h2, routed, g2, wts, *, tm, per_token_mod):
    B, S, D = x1.shape
    nj = S // tm
    tok = lambda b, j: (b, j, 0)
    mod = (lambda b, j: (b, j, 0)) if per_token_mod else (lambda b, j: (b, 0, 0))
    mod_rows = tm if per_token_mod else 1
    const = lambda b, j: (0, 0)
    return pl.pallas_call(
        _final_body,
        out_shape=jax.ShapeDtypeStruct((B, S, D), F32),
        grid=(B, nj),
        in_specs=[pl.BlockSpec((None, tm, D), tok), pl.BlockSpec((None, tm, D), tok), pl.BlockSpec((None, tm, D), tok),
                  pl.BlockSpec((None, mod_rows, D), mod)] + [pl.BlockSpec(w.shape, const) for w in wts],
        out_specs=pl.BlockSpec((None, tm, D), tok),
        compiler_params=_cparams("arbitrary", "arbitrary"),
        name="final",
    )(x1, h2, routed, g2, *wts)


def _ffn_tail(x1, h2, lt, g2, wts_moe, wts_final, bias_col, *, tm_moe, tm, per_token_mod):
    B, S, D = x1.shape
    T = B * S
    wgu, wd = wts_moe
    n_slots = TOP_K * tm_moe + N_EXPERTS * MOE_CHUNK
    dest, wk, cnt, cst = _route(lt, bias_col, tw=tm_moe)
    slot_tok, slot_w = _slot_lists(dest, wk, tw=tm_moe, n_slots=n_slots)
    routed = _moe(h2.reshape(T, D), cnt[:, :, 0], cst[:, :, 0], slot_tok, slot_w, wgu, wd, tm=tm_moe)
    return _final(x1, h2, routed.reshape(B, S, D), g2, wts_final, tm=tm, per_token_mod=per_token_mod)


def _w_in_layout(w_in):
    qb0 = Q_A + 2 * KV_A
    kb0 = qb0 + QKV_B
    vb0 = kb0 + QKV_B
    qs = HEAD_DIM ** -0.5
    cols = [w_in[:, :Q_A] * qs, w_in[:, Q_A:C_A]]
    for g in range(N_DIL):
        cols += [w_in[:, qb0 + g * O_B:qb0 + (g + 1) * O_B] * qs,
                 w_in[:, kb0 + g * O_B:kb0 + (g + 1) * O_B],
                 w_in[:, vb0 + g * O_B:vb0 + (g + 1) * O_B]]
    cols.append(w_in[:, vb0 + QKV_B:])
    return jnp.concatenate(cols, axis=1).astype(BF16)


def _merge_weights(w_br_swa, w_br_dil, w_out, ln1_g, ln1_b, w_router):
    wr_t = w_router.T
    wrh = wr_t.astype(BF16)
    wrl = (wr_t - wrh.astype(F32)).astype(BF16)
    return (w_br_swa.astype(BF16), w_br_dil.astype(BF16), w_out.astype(BF16),
            ln1_g.reshape(1, -1), ln1_b.reshape(1, -1), wrh, wrl)


def _final_weights(w_sh_gate, w_sh_up, w_sh_down, ln2_g, ln2_b):
    return (jnp.concatenate([w_sh_gate, w_sh_up], axis=-1).astype(BF16), w_sh_down.astype(BF16),
            ln2_g.reshape(1, -1), ln2_b.reshape(1, -1))


def _inproj_sample_body(x_ref, sh_ref, sc_ref, wqkv_t_ref, wg_ref, zt_ref, g_ref):
    h = (_ln(x_ref[...]) * (1.0 + sc_ref[...]) + sh_ref[...]).astype(BF16)
    zt_ref[...] = _dot_nt(wqkv_t_ref[...], h)
    g_ref[...] = jax.nn.sigmoid(_dot(h, wg_ref[...])).astype(BF16)


def _inproj_sample(x, shift, scale, w_qkv_t, w_gates):
    n, d = x.shape
    full = lambda s: pl.BlockSpec(s, lambda i: (0,) * len(s))
    return pl.pallas_call(
        _inproj_sample_body,
        out_shape=(jax.ShapeDtypeStruct((N_QKV, n), F32), jax.ShapeDtypeStruct((n, 2 * d), BF16)),
        grid=(1,),
        in_specs=[full((n, d)), full((n, d)), full((n, d)), full((N_QKV, d)), full((d, 2 * d))],
        out_specs=(full((N_QKV, n)), full((n, 2 * d))),
        compiler_params=_cparams("arbitrary"),
        name="inproj_sample",
    )(x, shift, scale, w_qkv_t, w_gates)


def _sample_window(c_ref, o_ref, qs, k_new, v_new, *, step, slopes, sinks):
    n_kv = c_ref.shape[1]
    W = c_ref.shape[-1]
    group = len(qs) // n_kv
    pos = lax.broadcasted_iota(jnp.int32, (1, W), 1)
    dist = (W - pos).astype(F32)
    valid = (pos & (step - 1)) == 0
    newest = pos == W - 1
    outs, lses = [], []
    for hk in range(n_kv):
        kt = c_ref[0, hk]
        vt = c_ref[1, hk]
        kn, vn = k_new[hk], v_new[hk]
        o_ref[0, hk] = jnp.where(newest, kn, pltpu.roll(kt, W - 1, 1))
        o_ref[1, hk] = jnp.where(newest, vn, pltpu.roll(vt, W - 1, 1))
        for g in range(group):
            h = hk * group + g
            q = qs[h]
            s_c = jnp.sum(q * kt, axis=0, keepdims=True) - slopes[h] * dist
            s_c = jnp.where(valid, s_c, NEG_INF)
            s_n = jnp.sum(q * kn, axis=0, keepdims=True)
            m = jnp.maximum(jnp.max(s_c, axis=1, keepdims=True), s_n)
            if sinks is not None:
                m = jnp.maximum(m, sinks[h])
            p_c = jnp.exp(s_c - m)
            p_n = jnp.exp(s_n - m)
            den = jnp.sum(p_c, axis=1, keepdims=True) + p_n
            if sinks is not None:
                den = den + jnp.exp(sinks[h] - m)
            outs.append((jnp.sum(p_c * vt, axis=1, keepdims=True) + p_n * vn) / den)
            lses.append(m + jnp.log(den))
    return outs, lses


def _sample_attn_body(sink_ref, zt_ref, cs_ref, c1_ref, c2_ref, c3_ref, ot_ref, os_ref, o1_ref, o2_ref, o3_ref):
    b = pl.program_id(0)
    n = zt_ref.shape[1]
    lane = lax.broadcasted_iota(jnp.int32, (N_QKV, n), 1)
    col = jnp.sum(jnp.where(lane == b, zt_ref[...], 0.0), axis=1, keepdims=True)
    heads = lambda off, cnt: [col[off + h * HEAD_DIM:off + (h + 1) * HEAD_DIM, :] for h in range(cnt)]
    sinks = [sink_ref[h] for h in range(SWA_Q_HEADS)]
    o_swa, _ = _sample_window(cs_ref, os_ref, heads(0, SWA_Q_HEADS), heads(Q_A, SWA_KV_HEADS),
                              heads(Q_A + KV_A, SWA_KV_HEADS), step=1, slopes=SWA_SLOPES, sinks=sinks)
    outs, lses = [], []
    for g, (c_ref, o_ref) in enumerate(((c1_ref, o1_ref), (c2_ref, o2_ref), (c3_ref, o3_ref))):
        base = C_A + g * C_B
        o, l = _sample_window(c_ref, o_ref, heads(base, DIL_HEADS), heads(base + O_B, DIL_HEADS),
                              heads(base + 2 * O_B, DIL_HEADS), step=DIL_PAIRS[g][1],
                              slopes=DIL_SLOPES[g * DIL_HEADS:(g + 1) * DIL_HEADS], sinks=None)
        outs.append(o)
        lses.append(l)
    o_dil = []
    for h in range(DIL_HEADS):
        m = functools.reduce(jnp.maximum, [lses[g][h] for g in range(N_DIL)])
        es = [jnp.exp(lses[g][h] - m) for g in range(N_DIL)]
        o_dil.append(sum(es[g] * outs[g][h] for g in range(N_DIL)) / sum(es))
    ocol = jnp.concatenate(o_swa + o_dil, axis=0)

    @pl.when(b == 0)
    def _():
        ot_ref[...] = jnp.zeros(ot_ref.shape, F32)

    lane_o = lax.broadcasted_iota(jnp.int32, ot_ref.shape, 1)
    ot_ref[...] = jnp.where(lane_o == b, ocol, ot_ref[...])


def _sample_attn(zt, sinks, caches):
    n = zt.shape[1]
    blk = lambda c: pl.BlockSpec((None,) + c.shape[1:], lambda b, *_: (b, 0, 0, 0, 0))
    return pl.pallas_call(
        _sample_attn_body,
        out_shape=(jax.ShapeDtypeStruct((Q_A + O_B, n), F32),) + tuple(
            jax.ShapeDtypeStruct(c.shape, F32) for c in caches),
        grid_spec=pltpu.PrefetchScalarGridSpec(
            num_scalar_prefetch=1,
            grid=(n,),
            in_specs=[pl.BlockSpec(zt.shape, lambda b, *_: (0, 0))] + [blk(c) for c in caches],
            out_specs=(pl.BlockSpec((Q_A + O_B, n), lambda b, *_: (0, 0)),) + tuple(blk(c) for c in caches)),
        compiler_params=_cparams("arbitrary"),
        name="sample_attn",
    )(sinks, zt, *caches)


def _pos_minor(cache):
    return jnp.transpose(cache, (0, 2, 3, 4, 1))


def _pos_major(state):
    return jnp.transpose(state, (0, 4, 1, 2, 3))[None]


def kernel(x_prompt, x_sample, cache_swa_kv, cache_dil1_kv, cache_dil2_kv, cache_dil3_kv, c_prompt, c_sample,
           w_ada, b_ada, w_in, attn_sinks, w_br_swa, w_br_dil, w_out, ln1_g, ln1_b, w_router, router_bias,
           w_exp_gate, w_exp_up, w_exp_down, w_sh_gate, w_sh_up, w_sh_down, ln2_g, ln2_b):
    assert w_ada.shape[0] == DEPTH == 1
    l = 0
    B, S, D = x_prompt.shape
    Bs = x_sample.shape[0]
    assert x_sample.shape[1] == 1

    w_in_p = _w_in_layout(w_in[l])
    wts_merge = _merge_weights(w_br_swa[l], w_br_dil[l], w_out[l], ln1_g[l], ln1_b[l], w_router[l])
    wts_moe = (jnp.concatenate([w_exp_gate[l], w_exp_up[l]], axis=-1).astype(BF16), w_exp_down[l].astype(BF16))
    wts_final = _final_weights(w_sh_gate[l], w_sh_up[l], w_sh_down[l], ln2_g[l], ln2_b[l])
    bias_col = router_bias[l].reshape(N_EXPERTS, 1)
    sinks = attn_sinks[l]

    mod = _ada(jnp.concatenate([c_prompt, c_sample], axis=0), w_ada[l], b_ada[l])
    mod_p = [m[:, None, :] for m in jnp.split(mod[:B], 6, axis=-1)]
    mod_s = [m[None] for m in jnp.split(mod[B:], 6, axis=-1)]

    sh1, sc1, g1, sh2, sc2, g2 = mod_p
    qa, qb0, qb1, qb2, sg, s2, s3, s4, s5 = _inproj_prompt(x_prompt, sh1, sc1, w_in_p)
    (o_swa,) = _band_attn(qa, sinks, n_kv=SWA_KV_HEADS, group=SWA_GROUP, dil=1, slopes=SWA_SLOPES,
                          use_sink=True, emit_lse=False)
    ods, lses = [], []
    for g, qb in enumerate((qb0, qb1, qb2)):
        o, ls = _band_attn(qb, sinks, n_kv=DIL_HEADS, group=1, dil=DIL_PAIRS[g][1],
                           slopes=DIL_SLOPES[g * DIL_HEADS:(g + 1) * DIL_HEADS], use_sink=False, emit_lse=True)
        ods.append(o)
        lses.append(ls)
    x1, h2, lt = _merge(x_prompt, o_swa, ods, lses, sg, g1, sh2, sc2, wts_merge, tm=512, per_token_mod=False)
    y_prompt = _ffn_tail(x1, h2, lt, g2, wts_moe, wts_final, bias_col, tm_moe=S, tm=512, per_token_mod=False)
    states_p = [_pos_major(s.reshape(B, 2, nh, HEAD_DIM, s.shape[-1]))
                for s, nh in ((s2, SWA_KV_HEADS), (s3, DIL_HEADS), (s4, DIL_HEADS), (s5, DIL_HEADS))]

    sh1, sc1, g1, sh2, sc2, g2 = mod_s
    xs = x_sample.reshape(Bs, D)
    zt, sg_s = _inproj_sample(xs, sh1[0], sc1[0], w_in_p[:, :N_QKV].T, w_in_p[:, N_QKV:])
    caches = [_pos_minor(c[l]) for c in (cache_swa_kv, cache_dil1_kv, cache_dil2_kv, cache_dil3_kv)]
    ot, *rolled = _sample_attn(zt, sinks, caches)
    o_s = ot.T[None]
    x1, h2, lt = _merge(xs[None], o_s[..., :Q_A], [o_s[..., Q_A:]], [], sg_s[None], g1, sh2, sc2, wts_merge,
                        tm=Bs, per_token_mod=True)
    y_sample = _ffn_tail(x1, h2, lt, g2, wts_moe, wts_final, bias_col, tm_moe=Bs, tm=Bs, per_token_mod=True)
    states_s = [_pos_major(r) for r in rolled]

    return (y_prompt, y_sample.reshape(Bs, 1, D), *states_p, *states_s)
```

```python
import functools

import jax
import jax.numpy as jnp
import numpy as np
from jax import lax
from jax.experimental import pallas as pl
from jax.experimental.pallas import tpu as pltpu

F32 = jnp.float32
BF16 = jnp.bfloat16

D_MODEL = 1024
HEAD_DIM = 64
SWA_WINDOW = 128
SWA_Q_HEADS = 8
SWA_KV_HEADS = 2
SWA_GROUP = SWA_Q_HEADS // SWA_KV_HEADS
DIL_PAIRS = ((128, 1), (512, 4), (2048, 16))
N_DIL = len(DIL_PAIRS)
DIL_HEADS = 4
BLOCK = 128
Q_A = SWA_Q_HEADS * HEAD_DIM
KV_A = SWA_KV_HEADS * HEAD_DIM
QKV_B = N_DIL * DIL_HEADS * HEAD_DIM
O_B = DIL_HEADS * HEAD_DIM
C_A = Q_A + 2 * KV_A
C_B = 3 * O_B
N_QKV = C_A + N_DIL * C_B
N_IN = N_QKV + 2 * D_MODEL
N_EXPERTS = 64
TOP_K = 8
N_EXPERT_GROUPS = 8
EXPERTS_PER_GROUP = N_EXPERTS // N_EXPERT_GROUPS
TOPK_GROUPS = 4
D_EXPERT = 256
D_SHARED = 256
ROUTED_SCALE = 2.5
LN_EPS = 1e-5
NEG_INF = -1e30
PAST_LEN = 16384
DEPTH = 1
ALPHA = (2.0 * DEPTH) ** 0.25

LANES = 128
SUBLANES = 8
VMEM_LIMIT_BYTES = 56 * 1024 * 1024
MOE_CHUNK = 128
ROW_STRIDE = MOE_CHUNK + SUBLANES

SWA_SLOPES = tuple(float(v) for v in np.asarray(2.0 ** (-8.0 * np.arange(1, SWA_Q_HEADS + 1) / SWA_Q_HEADS), np.float32))
DIL_SLOPES = tuple(float(v) for v in np.asarray(
    2.0 ** (-8.0 * np.arange(1, N_DIL * DIL_HEADS + 1) / (N_DIL * DIL_HEADS)), np.float32))


def _cparams(*sem):
    return pltpu.CompilerParams(dimension_semantics=sem, vmem_limit_bytes=VMEM_LIMIT_BYTES)


def _ln(x):
    mu = jnp.mean(x, axis=-1, keepdims=True)
    xc = x - mu
    var = jnp.mean(xc * xc, axis=-1, keepdims=True)
    return xc * lax.rsqrt(var + LN_EPS)


def _silu(x):
    return x * jax.nn.sigmoid(x)


def _dot(a, b):
    return jnp.dot(a, b, preferred_element_type=F32)


def _dot_nt(a, b):
    return lax.dot_general(a, b, (((1,), (1,)), ((), ())), preferred_element_type=F32)


def _ada_body(c_ref, w_ref, b_ref, o_ref):
    s = _silu(c_ref[...]).astype(BF16)
    o_ref[...] = _dot(s, w_ref[...].astype(BF16)) + b_ref[...]


def _ada(c, w_ada, b_ada):
    rows, d = c.shape
    n = w_ada.shape[1]
    tn = 1024
    return pl.pallas_call(
        _ada_body,
        out_shape=jax.ShapeDtypeStruct((rows, n), F32),
        grid=(n // tn,),
        in_specs=[pl.BlockSpec((rows, d), lambda j: (0, 0)),
                  pl.BlockSpec((d, tn), lambda j: (0, j)),
                  pl.BlockSpec((1, tn), lambda j: (0, j))],
        out_specs=pl.BlockSpec((rows, tn), lambda j: (0, j)),
        compiler_params=_cparams("arbitrary"),
        name="ada_mod",
    )(c, w_ada, b_ada.reshape(1, n))


def _inproj_body(x_ref, sh_ref, sc_ref, w_ref, qa_ref, qb0_ref, qb1_ref, qb2_ref, g_ref,
                 s2_ref, s3_ref, s4_ref, s5_ref, *, tm):
    j = pl.program_id(1)
    last = pl.num_programs(1) - 1
    h = (_ln(x_ref[...]) * (1.0 + sc_ref[...]) + sh_ref[...]).astype(BF16)
    za = _dot(h, w_ref[:, 0:C_A])
    qa_ref[...] = za.astype(BF16)
    zb = []
    for g, ref in enumerate((qb0_ref, qb1_ref, qb2_ref)):
        z = _dot(h, w_ref[:, C_A + g * C_B:C_A + (g + 1) * C_B])
        ref[...] = z.astype(BF16)
        zb.append(z)
    gw = 512
    for c in range(2 * D_MODEL // gw):
        z = _dot(h, w_ref[:, N_QKV + c * gw:N_QKV + (c + 1) * gw])
        g_ref[:, c * gw:(c + 1) * gw] = jax.nn.sigmoid(z).astype(BF16)
    s5_ref[...] = zb[2][:, O_B:].T

    @pl.when(j == last)
    def _():
        s4_ref[...] = zb[1][:, O_B:].T
        s3_ref[...] = zb[0][tm - BLOCK:, O_B:].T
        s2_ref[...] = za[tm - BLOCK:, Q_A:].T


def _inproj_prompt(x, shift, scale, w_in_p):
    B, S, D = x.shape
    tm = 512
    nj = S // tm
    assert S % tm == 0 and DIL_PAIRS[1][0] == tm and DIL_PAIRS[2][0] == S and SWA_WINDOW == BLOCK
    row = lambda b, j: (b, 0, 0)
    tok = lambda b, j: (b, j, 0)
    outs = pl.pallas_call(
        functools.partial(_inproj_body, tm=tm),
        out_shape=(jax.ShapeDtypeStruct((B, S, C_A), BF16),
                   jax.ShapeDtypeStruct((B, S, C_B), BF16),
                   jax.ShapeDtypeStruct((B, S, C_B), BF16),
                   jax.ShapeDtypeStruct((B, S, C_B), BF16),
                   jax.ShapeDtypeStruct((B, S, 2 * D), BF16),
                   jax.ShapeDtypeStruct((B, 2 * KV_A, BLOCK), F32),
                   jax.ShapeDtypeStruct((B, 2 * O_B, BLOCK), F32),
                   jax.ShapeDtypeStruct((B, 2 * O_B, tm), F32),
                   jax.ShapeDtypeStruct((B, 2 * O_B, S), F32)),
        grid=(B, nj),
        in_specs=[pl.BlockSpec((None, tm, D), tok),
                  pl.BlockSpec((None, 1, D), row),
                  pl.BlockSpec((None, 1, D), row),
                  pl.BlockSpec((D, N_IN), lambda b, j: (0, 0))],
        out_specs=(pl.BlockSpec((None, tm, C_A), tok),
                   pl.BlockSpec((None, tm, C_B), tok),
                   pl.BlockSpec((None, tm, C_B), tok),
                   pl.BlockSpec((None, tm, C_B), tok),
                   pl.BlockSpec((None, tm, 2 * D), tok),
                   pl.BlockSpec((None, 2 * KV_A, BLOCK), row),
                   pl.BlockSpec((None, 2 * O_B, BLOCK), row),
                   pl.BlockSpec((None, 2 * O_B, tm), row),
                   pl.BlockSpec((None, 2 * O_B, tm), lambda b, j: (b, 0, j))),
        compiler_params=_cparams("arbitrary", "arbitrary"),
        name="inproj_prompt",
    )(x, shift, scale, w_in_p)
    return outs


def _band_attn_body(sink_ref, cur_ref, *refs, n_kv, group, step, slopes, use_sink, emit_lse, nres, nqb,
                    has_prev):
    if has_prev:
        prev_ref, refs = refs[0], refs[1:]
    o_ref = refs[0]
    lse_ref = refs[1] if emit_lse else None
    i = pl.program_id(2)
    nq = n_kv * group
    C = (nq + 2 * n_kv) * HEAD_DIM
    ow = nq * HEAD_DIM
    nk = 2 * BLOCK if has_prev else BLOCK
    qi = lax.broadcasted_iota(jnp.int32, (BLOCK, nk), 0)
    kj = lax.broadcasted_iota(jnp.int32, (BLOCK, nk), 1)
    rel = qi + (nk - BLOCK) - kj
    band = (rel >= 0) & (rel <= BLOCK)
    dist = (rel * step).astype(F32)
    k_off = nq * HEAD_DIM
    v_off = k_off + n_kv * HEAD_DIM
    for r in range(nres):
        c0 = r * C
        for qb in range(nqb):
            rows = slice(qb * BLOCK, (qb + 1) * BLOCK)
            if has_prev and qb == 0:
                valid = band & ((kj >= BLOCK) | (i > 0))
            else:
                valid = band
            outs, lses = [], []
            for hk in range(n_kv):
                ks = slice(c0 + k_off + hk * HEAD_DIM, c0 + k_off + (hk + 1) * HEAD_DIM)
                vs = slice(c0 + v_off + hk * HEAD_DIM, c0 + v_off + (hk + 1) * HEAD_DIM)
                k = cur_ref[rows, ks]
                v = cur_ref[rows, vs]
                if has_prev:
                    if qb == 0:
                        kp, vp = prev_ref[:, ks], prev_ref[:, vs]
                    else:
                        prows = slice((qb - 1) * BLOCK, qb * BLOCK)
                        kp, vp = cur_ref[prows, ks], cur_ref[prows, vs]
                    k = jnp.concatenate([kp, k], axis=0)
                    v = jnp.concatenate([vp, v], axis=0)
                for g in range(group):
                    h = hk * group + g
                    q = cur_ref[rows, c0 + h * HEAD_DIM:c0 + (h + 1) * HEAD_DIM]
                    s = _dot_nt(q, k) - slopes[h] * dist
                    s = jnp.where(valid, s, NEG_INF)
                    m = jnp.max(s, axis=-1, keepdims=True)
                    if use_sink:
                        sink = sink_ref[h]
                        m = jnp.maximum(m, sink)
                    p = jnp.exp(s - m)
                    denom = jnp.sum(p, axis=-1, keepdims=True)
                    if use_sink:
                        denom = denom + jnp.exp(sink - m)
                    outs.append(_dot(p.astype(BF16), v) / denom)
                    lses.append(m + jnp.log(denom))
            o_ref[rows, r * ow:(r + 1) * ow] = jnp.concatenate(outs, axis=-1).astype(BF16)
            if emit_lse:
                lse_ref[rows, r * ow:(r + 1) * ow] = jnp.concatenate(
                    [jnp.broadcast_to(l, (BLOCK, HEAD_DIM)) for l in lses], axis=-1)


def _band_attn(qkv, sinks, *, n_kv, group, dil, slopes, use_sink, emit_lse):
    B, S, C = qkv.shape
    L = S // dil
    nb = L // BLOCK
    nq = n_kv * group
    ow = nq * HEAD_DIM
    nqb = min(nb, 4)
    nres = min(dil, max(1, 4 // nqb))
    has_prev = nb > 1
    view = qkv.reshape(B, L, dil * C)
    blk = lambda b, r, i, *_: (b, i, r)
    out_shape = [jax.ShapeDtypeStruct((B, L, dil * ow), BF16)]
    out_specs = [pl.BlockSpec((None, nqb * BLOCK, nres * ow), blk)]
    if emit_lse:
        out_shape.append(jax.ShapeDtypeStruct((B, L, dil * ow), F32))
        out_specs.append(pl.BlockSpec((None, nqb * BLOCK, nres * ow), blk))
    in_specs = [pl.BlockSpec((None, nqb * BLOCK, nres * C), blk)]
    args = [sinks, view]
    if has_prev:
        in_specs.append(pl.BlockSpec((None, BLOCK, nres * C),
                                     lambda b, r, i, *_: (b, jnp.maximum(i * nqb - 1, 0), r)))
        args.append(view)
    res = pl.pallas_call(
        functools.partial(_band_attn_body, n_kv=n_kv, group=group, step=dil, slopes=slopes, use_sink=use_sink,
                          emit_lse=emit_lse, nres=nres, nqb=nqb, has_prev=has_prev),
        out_shape=tuple(out_shape),
        grid_spec=pltpu.PrefetchScalarGridSpec(
            num_scalar_prefetch=1,
            grid=(B, dil // nres, nb // nqb),
            in_specs=in_specs,
            out_specs=tuple(out_specs)),
        compiler_params=_cparams("arbitrary", "arbitrary", "arbitrary"),
        name=f"band_attn_d{dil}",
    )(*args)
    return tuple(r.reshape(B, S, ow) for r in res)


def _split_bf16(x):
    hi = x.astype(BF16)
    lo = (x - hi.astype(F32)).astype(BF16)
    return hi, lo


def _merge_body(*refs, n_groups):
    x_ref, oswa_ref = refs[0], refs[1]
    od_refs = refs[2:2 + n_groups]
    lse_refs = refs[2 + n_groups:2 + 2 * n_groups] if n_groups > 1 else ()
    k = 2 + (2 * n_groups if n_groups > 1 else 1)
    (sg_ref, g1_ref, sh2_ref, sc2_ref, wa_ref, wb_ref, wo_ref, lg_ref, lb_ref, wrh_ref, wrl_ref,
     x1_ref, h2_ref, lt_ref) = refs[k:]
    if n_groups > 1:
        ls = [r[...] for r in lse_refs]
        m = functools.reduce(jnp.maximum, ls)
        es = [jnp.exp(l - m) for l in ls]
        num = sum(e * r[...].astype(F32) for e, r in zip(es, od_refs))
        odil = num / sum(es)
    else:
        odil = od_refs[0][...].astype(F32)
    ya = _dot(oswa_ref[...].astype(BF16), wa_ref[...])
    yb = _dot(odil.astype(BF16), wb_ref[...])
    d = ya.shape[-1]
    u = sg_ref[:, :d].astype(F32) * ya + sg_ref[:, d:].astype(F32) * yb
    mix = _dot(u.astype(BF16), wo_ref[...])
    x1 = _ln(ALPHA * x_ref[...] + g1_ref[...] * mix) * lg_ref[...] + lb_ref[...]
    x1_ref[...] = x1
    h2 = _ln(x1) * (1.0 + sc2_ref[...]) + sh2_ref[...]
    hi, lo = _split_bf16(h2)
    h2_ref[...] = hi
    wrh = wrh_ref[...]
    lt_ref[...] = _dot_nt(wrh, hi) + _dot_nt(wrh, lo) + _dot_nt(wrl_ref[...], hi)


def _merge(x, oswa, ods, lses, sg, g1, sh2, sc2, wts, *, tm, per_token_mod):
    B, S, D = x.shape
    nj = S // tm
    n_groups = len(ods)
    wa, wb, wo, lg, lb, wrh, wrl = wts
    tok = lambda b, j: (b, j, 0)
    mod = (lambda b, j: (b, j, 0)) if per_token_mod else (lambda b, j: (b, 0, 0))
    mod_rows = tm if per_token_mod else 1
    const = lambda b, j: (0, 0)
    in_specs = [pl.BlockSpec((None, tm, D), tok), pl.BlockSpec((None, tm, Q_A), tok)]
    in_specs += [pl.BlockSpec((None, tm, O_B), tok)] * n_groups
    if n_groups > 1:
        in_specs += [pl.BlockSpec((None, tm, O_B), tok)] * n_groups
    in_specs += [pl.BlockSpec((None, tm, 2 * D), tok)]
    in_specs += [pl.BlockSpec((None, mod_rows, D), mod)] * 3
    in_specs += [pl.BlockSpec(w.shape, const) for w in wts]
    args = [x, oswa, *ods, *(lses if n_groups > 1 else ()), sg, g1, sh2, sc2, *wts]
    return pl.pallas_call(
        functools.partial(_merge_body, n_groups=n_groups),
        out_shape=(jax.ShapeDtypeStruct((B, S, D), F32),
                   jax.ShapeDtypeStruct((B, S, D), BF16),
                   jax.ShapeDtypeStruct((N_EXPERTS, B * S), F32)),
        grid=(B, nj),
        in_specs=in_specs,
        out_specs=(pl.BlockSpec((None, tm, D), tok),
                   pl.BlockSpec((None, tm, D), tok),
                   pl.BlockSpec((N_EXPERTS, tm), lambda b, j: (0, b * nj + j))),
        compiler_params=_cparams("arbitrary", "arbitrary"),
        name="merge",
    )(*args)


def _route_body(lt_ref, bias_ref, dest_ref, slot_ref, wn_ref, cnt_ref, cst_ref, *, tw):
    G = EXPERTS_PER_GROUP
    sub = lax.broadcasted_iota(jnp.int32, (G, tw), 0)
    ninf = -jnp.inf
    scores, biased = [], []
    for g in range(N_EXPERT_GROUPS):
        s = jax.nn.sigmoid(lt_ref[g * G:(g + 1) * G, :])
        scores.append(s)
        biased.append(s + bias_ref[g * G:(g + 1) * G, :])
    gs = []
    for g in range(N_EXPERT_GROUPS):
        m1 = jnp.max(biased[g], axis=0, keepdims=True)
        i1 = jnp.min(jnp.where(biased[g] == m1, sub, G), axis=0, keepdims=True)
        m2 = jnp.max(jnp.where(sub == i1, ninf, biased[g]), axis=0, keepdims=True)
        gs.append(m1 + m2)
    gsel = [jnp.zeros((1, tw), jnp.bool_) for _ in range(N_EXPERT_GROUPS)]
    for _ in range(TOPK_GROUPS):
        mx = functools.reduce(jnp.maximum, gs)
        ix = functools.reduce(jnp.minimum, [jnp.where(gs[g] == mx, g, N_EXPERT_GROUPS) for g in range(N_EXPERT_GROUPS)])
        for g in range(N_EXPERT_GROUPS):
            pick = ix == g
            gsel[g] = gsel[g] | pick
            gs[g] = jnp.where(pick, ninf, gs[g])
    work = [jnp.where(gsel[g], biased[g], ninf) for g in range(N_EXPERT_GROUPS)]
    chosen = [jnp.zeros((G, tw), jnp.bool_) for _ in range(N_EXPERT_GROUPS)]
    idxs = []
    for _ in range(TOP_K):
        mx = functools.reduce(jnp.maximum, [jnp.max(w, axis=0, keepdims=True) for w in work])
        ix = functools.reduce(jnp.minimum, [
            jnp.min(jnp.where(work[g] == mx, sub + g * G, N_EXPERTS), axis=0, keepdims=True)
            for g in range(N_EXPERT_GROUPS)])
        idxs.append(ix)
        for g in range(N_EXPERT_GROUPS):
            pick = (sub + g * G) == ix
            chosen[g] = chosen[g] | pick
            work[g] = jnp.where(pick, ninf, work[g])
    wsel = [jnp.where(chosen[g], scores[g], 0.0) for g in range(N_EXPERT_GROUPS)]
    wsum = sum(jnp.sum(w, axis=0, keepdims=True) for w in wsel)
    wn = [w / wsum * ROUTED_SCALE for w in wsel]
    cm = jnp.concatenate([c.astype(F32) for c in chosen], axis=0)
    r_i = lax.broadcasted_iota(jnp.int32, (LANES, LANES), 0)
    c_i = lax.broadcasted_iota(jnp.int32, (LANES, LANES), 1)
    upper = (r_i < c_i).astype(BF16)
    carry = jnp.zeros((N_EXPERTS, 1), F32)
    ranks = []
    for c in range(tw // LANES):
        blk = cm[:, c * LANES:(c + 1) * LANES]
        ranks.append(_dot(blk.astype(BF16), upper) + carry)
        carry = carry + jnp.sum(blk, axis=1, keepdims=True)
    rank = jnp.concatenate(ranks, axis=1)
    nchunk = jnp.floor((carry + (MOE_CHUNK - 1)) * (1.0 / MOE_CHUNK))
    e_r = lax.broadcasted_iota(jnp.int32, (N_EXPERTS, N_EXPERTS), 0)
    e_c = lax.broadcasted_iota(jnp.int32, (N_EXPERTS, N_EXPERTS), 1)
    lower = (e_c < e_r).astype(BF16)
    cstart = _dot(lower, jnp.broadcast_to(nchunk, (N_EXPERTS, LANES)).astype(BF16))
    slot = cstart[:, :1] * float(MOE_CHUNK) + rank
    dests = []
    for k in range(TOP_K):
        d = jnp.zeros((1, tw), F32)
        for g in range(N_EXPERT_GROUPS):
            pick = (sub + g * G) == idxs[k]
            d = d + jnp.sum(jnp.where(pick, slot[g * G:(g + 1) * G, :], 0.0), axis=0, keepdims=True)
        dests.append(d)
    dest_ref[...] = jnp.concatenate(dests, axis=0).astype(jnp.int32)
    slot_ref[...] = jnp.where(cm > 0.0, slot, -1.0)
    wn_ref[...] = jnp.concatenate(wn, axis=0)
    cnt_ref[...] = jnp.broadcast_to(carry, (N_EXPERTS, LANES)).astype(jnp.int32)
    cst_ref[...] = cstart.astype(jnp.int32)


def _route(lt, bias_col, *, tw):
    E, T = lt.shape
    nt = T // tw
    assert tw // MOE_CHUNK + N_EXPERTS <= 256
    return pl.pallas_call(
        functools.partial(_route_body, tw=tw),
        out_shape=(jax.ShapeDtypeStruct((nt, TOP_K, tw), jnp.int32),
                   jax.ShapeDtypeStruct((E, T), F32),
                   jax.ShapeDtypeStruct((E, T), F32),
                   jax.ShapeDtypeStruct((nt, E, LANES), jnp.int32),
                   jax.ShapeDtypeStruct((nt, E, LANES), jnp.int32)),
        grid=(nt,),
        in_specs=[pl.BlockSpec((E, tw), lambda i: (0, i)),
                  pl.BlockSpec((E, 1), lambda i: (0, 0))],
        out_specs=(pl.BlockSpec((None, TOP_K, tw), lambda i: (i, 0, 0)),
                   pl.BlockSpec((E, tw), lambda i: (0, i)),
                   pl.BlockSpec((E, tw), lambda i: (0, i)),
                   pl.BlockSpec((None, E, LANES), lambda i: (i, 0, 0)),
                   pl.BlockSpec((None, E, LANES), lambda i: (i, 0, 0))),
        compiler_params=_cparams("arbitrary"),
        name="route",
    )(lt, bias_col)


def _moe_body(cnt_ref, cst_ref, dest_ref, slot_ref, wn_ref, h_ref, wgu_ref, wd_ref, o_ref,
              src_ref, acc_ref, xt_ref, yt_ref, tok_ref, *, tm):
    i = pl.program_id(0)
    e = pl.program_id(1)
    nchunks_lane = D_MODEL // LANES
    pad_row = tm * SUBLANES

    @pl.when(e == 0)
    def _():
        def relayout(a, c):
            rows = h_ref[pl.ds(pl.multiple_of(a * 16, 16), 16), :].astype(F32)
            for j in range(nchunks_lane):
                src_ref[pl.ds(a * (16 * SUBLANES) + j, 16, stride=SUBLANES), :] = rows[:, j * LANES:(j + 1) * LANES]
            return c
        lax.fori_loop(0, tm // 16, relayout, 0)
        src_ref[pl.ds(pad_row, SUBLANES), :] = jnp.zeros((SUBLANES, LANES), F32)
        acc_ref[...] = jnp.zeros(acc_ref.shape, F32)

        unroll = 4
        def invert(tb, c):
            for u in range(unroll):
                t = tb * unroll + u
                for k in range(TOP_K):
                    tok_ref[dest_ref[0, 0, k * tm + t]] = t * SUBLANES
            return c
        lax.fori_loop(0, tm // unroll, invert, 0)

    n = cnt_ref[i, e]
    c0 = cst_ref[i, e]
    nch = (n + (MOE_CHUNK - 1)) // MOE_CHUNK
    seg = c0 * MOE_CHUNK
    end = nch * MOE_CHUNK

    def fill_pad(kb, c):
        for u in range(SUBLANES):
            tok_ref[seg + jnp.maximum(end - 1 - (kb * SUBLANES + u), n)] = pad_row
        return c
    lax.fori_loop(0, (end - n + (SUBLANES - 1)) // SUBLANES, fill_pad, 0)

    row = e % SUBLANES
    slot_row = slot_ref[pl.ds(row, 1), :]
    wn_row = wn_ref[pl.ds(row, 1), :]
    sub = lax.broadcasted_iota(jnp.int32, (MOE_CHUNK, 1), 0)

    def chunk(c, carry):
        base = seg + c * MOE_CHUNK
        for mi in range(MOE_CHUNK):
            t8 = pl.multiple_of(tok_ref[base + mi], SUBLANES)
            xt_ref[pl.ds(mi, SUBLANES, stride=ROW_STRIDE), :] = src_ref[pl.ds(t8, SUBLANES), :]
        x = jnp.concatenate([xt_ref[pl.ds(j * ROW_STRIDE, MOE_CHUNK), :] for j in range(nchunks_lane)],
                            axis=-1).astype(BF16)
        gu = _dot(x, wgu_ref[...])
        a = _silu(gu[:, :D_EXPERT]) * gu[:, D_EXPERT:]
        wcol = jnp.sum(jnp.where(slot_row == (base + sub).astype(F32), wn_row, 0.0), axis=1, keepdims=True)
        y = _dot(a.astype(BF16), wd_ref[...]) * wcol
        for j in range(nchunks_lane):
            yt_ref[pl.ds(j * ROW_STRIDE, MOE_CHUNK), :] = y[:, j * LANES:(j + 1) * LANES]
        batch = 8
        for b in range(MOE_CHUNK // batch):
            vals = []
            for u in range(batch):
                mi = b * batch + u
                t8 = pl.multiple_of(tok_ref[base + mi], SUBLANES)
                vals.append((t8, acc_ref[pl.ds(t8, SUBLANES), :] + yt_ref[pl.ds(mi, SUBLANES, stride=ROW_STRIDE), :]))
            for t8, v in vals:
                acc_ref[pl.ds(t8, SUBLANES), :] = v
        return carry

    lax.fori_loop(0, nch, chunk, 0)

    @pl.when(e == pl.num_programs(1) - 1)
    def _():
        def writeback(a, c):
            for j in range(nchunks_lane):
                o_ref[pl.ds(pl.multiple_of(a * SUBLANES, SUBLANES), SUBLANES), j * LANES:(j + 1) * LANES] = (
                    acc_ref[pl.ds(a * (SUBLANES * SUBLANES) + j, SUBLANES, stride=SUBLANES), :])
            return c
        lax.fori_loop(0, tm // SUBLANES, writeback, 0)


def _moe(h2, cnt, cst, dest, slot, wn, wgu, wd, *, tm):
    T, D = h2.shape
    nt = T // tm
    n_slots = TOP_K * tm + N_EXPERTS * MOE_CHUNK
    stage_rows = SUBLANES * ROW_STRIDE
    return pl.pallas_call(
        functools.partial(_moe_body, tm=tm),
        out_shape=jax.ShapeDtypeStruct((T, D), F32),
        grid_spec=pltpu.PrefetchScalarGridSpec(
            num_scalar_prefetch=2,
            grid=(nt, N_EXPERTS),
            in_specs=[pl.BlockSpec((1, 1, TOP_K * tm), lambda i, e, *_: (i, 0, 0), memory_space=pltpu.SMEM),
                      pl.BlockSpec((SUBLANES, tm), lambda i, e, *_: (e // SUBLANES, i)),
                      pl.BlockSpec((SUBLANES, tm), lambda i, e, *_: (e // SUBLANES, i)),
                      pl.BlockSpec((tm, D), lambda i, e, *_: (i, 0)),
                      pl.BlockSpec((None, D, 2 * D_EXPERT), lambda i, e, *_: (e, 0, 0)),
                      pl.BlockSpec((None, D_EXPERT, D), lambda i, e, *_: (e, 0, 0))],
            out_specs=pl.BlockSpec((tm, D), lambda i, e, *_: (i, 0)),
            scratch_shapes=[pltpu.VMEM(((tm + 1) * SUBLANES, LANES), F32),
                            pltpu.VMEM(((tm + 1) * SUBLANES, LANES), F32),
                            pltpu.VMEM((stage_rows, LANES), F32),
                            pltpu.VMEM((stage_rows, LANES), F32),
                            pltpu.SMEM((n_slots,), jnp.int32)]),
        compiler_params=_cparams("arbitrary", "arbitrary"),
        name="moe_routed",
    )(cnt, cst, dest, slot, wn, h2, wgu, wd)


def _final_body(x1_ref, h2_ref, r_ref, g2_ref, wsgu_ref, wsd_ref, lg_ref, lb_ref, o_ref):
    gu = _dot(h2_ref[...], wsgu_ref[...])
    a = _silu(gu[:, :D_SHARED]) * gu[:, D_SHARED:]
    ffn = r_ref[...] + _dot(a.astype(BF16), wsd_ref[...])
    o_ref[...] = _ln(ALPHA * x1_ref[...] + g2_ref[...] * ffn) * lg_ref[...] + lb_ref[...]


def _final(x1, h2, routed, g2, wts, *, tm, per_token_mod):
    B, S, D = x1.shape
    nj = S // tm
    tok = lambda b, j: (b, j, 0)
    mod = (lambda b, j: (b, j, 0)) if per_token_mod else (lambda b, j: (b, 0, 0))
    mod_rows = tm if per_token_mod else 1
    const = lambda b, j: (0, 0)
    return pl.pallas_call(
        _final_body,
        out_shape=jax.ShapeDtypeStruct((B, S, D), F32),
        grid=(B, nj),
        in_specs=[pl.BlockSpec((None, tm, D), tok), pl.BlockSpec((None, tm, D), tok), pl.BlockSpec((None, tm, D), tok),
                  pl.BlockSpec((None, mod_rows, D), mod)] + [pl.BlockSpec(w.shape, const) for w in wts],
        out_specs=pl.BlockSpec((None, tm, D), tok),
        compiler_params=_cparams("arbitrary", "arbitrary"),
        name="final",
    )(x1, h2, routed, g2, *wts)


def _ffn_tail(x1, h2, lt, g2, wts_moe, wts_final, bias_col, *, tm_moe, tm, per_token_mod):
    B, S, D = x1.shape
    T = B * S
    wgu, wd = wts_moe
    dest, slot, wn, cnt, cst = _route(lt, bias_col, tw=tm_moe)
    routed = _moe(h2.reshape(T, D), cnt[:, :, 0], cst[:, :, 0], dest.reshape(T // tm_moe, 1, TOP_K * tm_moe),
                  slot, wn, wgu, wd, tm=tm_moe)
    return _final(x1, h2, routed.reshape(B, S, D), g2, wts_final, tm=tm, per_token_mod=per_token_mod)


def _w_in_layout(w_in):
    qb0 = Q_A + 2 * KV_A
    kb0 = qb0 + QKV_B
    vb0 = kb0 + QKV_B
    qs = HEAD_DIM ** -0.5
    cols = [w_in[:, :Q_A] * qs, w_in[:, Q_A:C_A]]
    for g in range(N_DIL):
        cols += [w_in[:, qb0 + g * O_B:qb0 + (g + 1) * O_B] * qs,
                 w_in[:, kb0 + g * O_B:kb0 + (g + 1) * O_B],
                 w_in[:, vb0 + g * O_B:vb0 + (g + 1) * O_B]]
    cols.append(w_in[:, vb0 + QKV_B:])
    return jnp.concatenate(cols, axis=1).astype(BF16)


def _merge_weights(w_br_swa, w_br_dil, w_out, ln1_g, ln1_b, w_router):
    wr_t = w_router.T
    wrh = wr_t.astype(BF16)
    wrl = (wr_t - wrh.astype(F32)).astype(BF16)
    return (w_br_swa.astype(BF16), w_br_dil.astype(BF16), w_out.astype(BF16),
            ln1_g.reshape(1, -1), ln1_b.reshape(1, -1), wrh, wrl)


def _final_weights(w_sh_gate, w_sh_up, w_sh_down, ln2_g, ln2_b):
    return (jnp.concatenate([w_sh_gate, w_sh_up], axis=-1).astype(BF16), w_sh_down.astype(BF16),
            ln2_g.reshape(1, -1), ln2_b.reshape(1, -1))


def _inproj_sample_body(x_ref, sh_ref, sc_ref, wqkv_t_ref, wg_ref, zt_ref, g_ref):
    h = (_ln(x_ref[...]) * (1.0 + sc_ref[...]) + sh_ref[...]).astype(BF16)
    zt_ref[...] = _dot_nt(wqkv_t_ref[...], h)
    g_ref[...] = jax.nn.sigmoid(_dot(h, wg_ref[...])).astype(BF16)


def _inproj_sample(x, shift, scale, w_qkv_t, w_gates):
    n, d = x.shape
    full = lambda s: pl.BlockSpec(s, lambda i: (0,) * len(s))
    return pl.pallas_call(
        _inproj_sample_body,
        out_shape=(jax.ShapeDtypeStruct((N_QKV, n), F32), jax.ShapeDtypeStruct((n, 2 * d), BF16)),
        grid=(1,),
        in_specs=[full((n, d)), full((n, d)), full((n, d)), full((N_QKV, d)), full((d, 2 * d))],
        out_specs=(full((N_QKV, n)), full((n, 2 * d))),
        compiler_params=_cparams("arbitrary"),
        name="inproj_sample",
    )(x, shift, scale, w_qkv_t, w_gates)


def _sample_window(c_ref, o_ref, qs, k_new, v_new, *, step, slopes, sinks):
    n_kv = c_ref.shape[1]
    W = c_ref.shape[-1]
    group = len(qs) // n_kv
    pos = lax.broadcasted_iota(jnp.int32, (1, W), 1)
    dist = (W - pos).astype(F32)
    valid = (pos & (step - 1)) == 0
    newest = pos == W - 1
    outs, lses = [], []
    for hk in range(n_kv):
        kt = c_ref[0, hk]
        vt = c_ref[1, hk]
        kn, vn = k_new[hk], v_new[hk]
        o_ref[0, hk] = jnp.where(newest, kn, pltpu.roll(kt, W - 1, 1))
        o_ref[1, hk] = jnp.where(newest, vn, pltpu.roll(vt, W - 1, 1))
        for g in range(group):
            h = hk * group + g
            q = qs[h]
            s_c = jnp.sum(q * kt, axis=0, keepdims=True) - slopes[h] * dist
            s_c = jnp.where(valid, s_c, NEG_INF)
            s_n = jnp.sum(q * kn, axis=0, keepdims=True)
            m = jnp.maximum(jnp.max(s_c, axis=1, keepdims=True), s_n)
            if sinks is not None:
                m = jnp.maximum(m, sinks[h])
            p_c = jnp.exp(s_c - m)
            p_n = jnp.exp(s_n - m)
            den = jnp.sum(p_c, axis=1, keepdims=True) + p_n
            if sinks is not None:
                den = den + jnp.exp(sinks[h] - m)
            outs.append((jnp.sum(p_c * vt, axis=1, keepdims=True) + p_n * vn) / den)
            lses.append(m + jnp.log(den))
    return outs, lses


def _sample_attn_body(sink_ref, zt_ref, cs_ref, c1_ref, c2_ref, c3_ref, ot_ref, os_ref, o1_ref, o2_ref, o3_ref):
    b = pl.program_id(0)
    n = zt_ref.shape[1]
    lane = lax.broadcasted_iota(jnp.int32, (N_QKV, n), 1)
    col = jnp.sum(jnp.where(lane == b, zt_ref[...], 0.0), axis=1, keepdims=True)
    heads = lambda off, cnt: [col[off + h * HEAD_DIM:off + (h + 1) * HEAD_DIM, :] for h in range(cnt)]
    sinks = [sink_ref[h] for h in range(SWA_Q_HEADS)]
    o_swa, _ = _sample_window(cs_ref, os_ref, heads(0, SWA_Q_HEADS), heads(Q_A, SWA_KV_HEADS),
                              heads(Q_A + KV_A, SWA_KV_HEADS), step=1, slopes=SWA_SLOPES, sinks=sinks)
    outs, lses = [], []
    for g, (c_ref, o_ref) in enumerate(((c1_ref, o1_ref), (c2_ref, o2_ref), (c3_ref, o3_ref))):
        base = C_A + g * C_B
        o, l = _sample_window(c_ref, o_ref, heads(base, DIL_HEADS), heads(base + O_B, DIL_HEADS),
                              heads(base + 2 * O_B, DIL_HEADS), step=DIL_PAIRS[g][1],
                              slopes=DIL_SLOPES[g * DIL_HEADS:(g + 1) * DIL_HEADS], sinks=None)
        outs.append(o)
        lses.append(l)
    o_dil = []
    for h in range(DIL_HEADS):
        m = functools.reduce(jnp.maximum, [lses[g][h] for g in range(N_DIL)])
        es = [jnp.exp(lses[g][h] - m) for g in range(N_DIL)]
        o_dil.append(sum(es[g] * outs[g][h] for g in range(N_DIL)) / sum(es))
    ocol = jnp.concatenate(o_swa + o_dil, axis=0)

    @pl.when(b == 0)
    def _():
        ot_ref[...] = jnp.zeros(ot_ref.shape, F32)

    lane_o = lax.broadcasted_iota(jnp.int32, ot_ref.shape, 1)
    ot_ref[...] = jnp.where(lane_o == b, ocol, ot_ref[...])


def _sample_attn(zt, sinks, caches):
    n = zt.shape[1]
    blk = lambda c: pl.BlockSpec((None,) + c.shape[1:], lambda b, *_: (b, 0, 0, 0, 0))
    return pl.pallas_call(
        _sample_attn_body,
        out_shape=(jax.ShapeDtypeStruct((Q_A + O_B, n), F32),) + tuple(
            jax.ShapeDtypeStruct(c.shape, F32) for c in caches),
        grid_spec=pltpu.PrefetchScalarGridSpec(
            num_scalar_prefetch=1,
            grid=(n,),
            in_specs=[pl.BlockSpec(zt.shape, lambda b, *_: (0, 0))] + [blk(c) for c in caches],
            out_specs=(pl.BlockSpec((Q_A + O_B, n), lambda b, *_: (0, 0)),) + tuple(blk(c) for c in caches)),
        compiler_params=_cparams("arbitrary"),
        name="sample_attn",
    )(sinks, zt, *caches)


def _pos_minor(cache):
    return jnp.transpose(cache, (0, 2, 3, 4, 1))


def _pos_major(state):
    return jnp.transpose(state, (0, 4, 1, 2, 3))[None]


def kernel(x_prompt, x_sample, cache_swa_kv, cache_dil1_kv, cache_dil2_kv, cache_dil3_kv, c_prompt, c_sample,
           w_ada, b_ada, w_in, attn_sinks, w_br_swa, w_br_dil, w_out, ln1_g, ln1_b, w_router, router_bias,
           w_exp_gate, w_exp_up, w_exp_down, w_sh_gate, w_sh_up, w_sh_down, ln2_g, ln2_b):
    assert w_ada.shape[0] == DEPTH == 1
    l = 0
    B, S, D = x_prompt.shape
    Bs = x_sample.shape[0]
    assert x_sample.shape[1] == 1

    w_in_p = _w_in_layout(w_in[l])
    wts_merge = _merge_weights(w_br_swa[l], w_br_dil[l], w_out[l], ln1_g[l], ln1_b[l], w_router[l])
    wts_moe = (jnp.concatenate([w_exp_gate[l], w_exp_up[l]], axis=-1).astype(BF16), w_exp_down[l].astype(BF16))
    wts_final = _final_weights(w_sh_gate[l], w_sh_up[l], w_sh_down[l], ln2_g[l], ln2_b[l])
    bias_col = router_bias[l].reshape(N_EXPERTS, 1)
    sinks = attn_sinks[l]

    mod = _ada(jnp.concatenate([c_prompt, c_sample], axis=0), w_ada[l], b_ada[l])
    mod_p = [m[:, None, :] for m in jnp.split(mod[:B], 6, axis=-1)]
    mod_s = [m[None] for m in jnp.split(mod[B:], 6, axis=-1)]

    sh1, sc1, g1, sh2, sc2, g2 = mod_p
    qa, qb0, qb1, qb2, sg, s2, s3, s4, s5 = _inproj_prompt(x_prompt, sh1, sc1, w_in_p)
    (o_swa,) = _band_attn(qa, sinks, n_kv=SWA_KV_HEADS, group=SWA_GROUP, dil=1, slopes=SWA_SLOPES,
                          use_sink=True, emit_lse=False)
    ods, lses = [], []
    for g, qb in enumerate((qb0, qb1, qb2)):
        o, ls = _band_attn(qb, sinks, n_kv=DIL_HEADS, group=1, dil=DIL_PAIRS[g][1],
                           slopes=DIL_SLOPES[g * DIL_HEADS:(g + 1) * DIL_HEADS], use_sink=False, emit_lse=True)
        ods.append(o)
        lses.append(ls)
    x1, h2, lt = _merge(x_prompt, o_swa, ods, lses, sg, g1, sh2, sc2, wts_merge, tm=512, per_token_mod=False)
    y_prompt = _ffn_tail(x1, h2, lt, g2, wts_moe, wts_final, bias_col, tm_moe=S, tm=512, per_token_mod=False)
    states_p = [_pos_major(s.reshape(B, 2, nh, HEAD_DIM, s.shape[-1]))
                for s, nh in ((s2, SWA_KV_HEADS), (s3, DIL_HEADS), (s4, DIL_HEADS), (s5, DIL_HEADS))]

    sh1, sc1, g1, sh2, sc2, g2 = mod_s
    xs = x_sample.reshape(Bs, D)
    zt, sg_s = _inproj_sample(xs, sh1[0], sc1[0], w_in_p[:, :N_QKV].T, w_in_p[:, N_QKV:])
    caches = [_pos_minor(c[l]) for c in (cache_swa_kv, cache_dil1_kv, cache_dil2_kv, cache_dil3_kv)]
    ot, *rolled = _sample_attn(zt, sinks, caches)
    o_s = ot.T[None]
    x1, h2, lt = _merge(xs[None], o_s[..., :Q_A], [o_s[..., Q_A:]], [], sg_s[None], g1, sh2, sc2, wts_merge,
                        tm=Bs, per_token_mod=True)
    y_sample = _ffn_tail(x1, h2, lt, g2, wts_moe, wts_final, bias_col, tm_moe=Bs, tm=Bs, per_token_mod=True)
    states_s = [_pos_major(r) for r in rolled]

    return (y_prompt, y_sample.reshape(Bs, 1, D), *states_p, *states_s)
```
